```python
import math
import jax, jax.numpy as jnp
from jax import lax
import numpy as np

D_MODEL = 1024
BATCH = 32
SEQ = 2048
DEPTH = 2

ATT_HEADS = 6
ATT_HEAD_DIM = 64
ATT_WIDTH = ATT_HEADS * ATT_HEAD_DIM
IDX_HEADS = 4
IDX_DIM = 64
TOPK_MAX = 256
QBLOCK = 32
POOL_GROUPS = 4
POOL_GROUP_DIM = 64
POOL_WIDTH = POOL_GROUPS * POOL_GROUP_DIM
POOL_WINDOWS = (2, 4, 8, 16)
GLA_HEADS = 4
GLA_DK = 48
GLA_DV = 96
GLA_KEY_WIDTH = GLA_HEADS * GLA_DK
GLA_VAL_WIDTH = GLA_HEADS * GLA_DV
GLA_GATE_RANK = 16
GLA_TAU = 16.0
GLA_CHUNK = 64

MIX_WIDTH = ATT_WIDTH + POOL_WIDTH + GLA_VAL_WIDTH
EPS = 1e-6

IN_SPLITS = (
    ATT_WIDTH, ATT_WIDTH, ATT_WIDTH, ATT_WIDTH,
    IDX_HEADS * IDX_DIM, IDX_DIM, IDX_HEADS,
    POOL_WIDTH, POOL_WIDTH,
    GLA_KEY_WIDTH, GLA_KEY_WIDTH, GLA_VAL_WIDTH,
    GLA_GATE_RANK, GLA_VAL_WIDTH,
)
IN_WIDTH = sum(IN_SPLITS)

kernel_name = "hybrid_dsa_pool_gla_parallel"


def rms_norm(x, gain):
    xf = x.astype(jnp.float32)
    y = xf * lax.rsqrt(jnp.mean(xf * xf, axis=-1, keepdims=True) + EPS)
    return (y * gain.astype(jnp.float32)).astype(x.dtype)


def split_columns(p):
    offs = np.cumsum(IN_SPLITS)[:-1].tolist()
    return jnp.split(p, offs, axis=-1)


def sparse_attention(q, k, v, iq, ik, iw):
    B, S, H, D = q.shape
    topk = min(TOPK_MAX, S // 4)
    nb = S // QBLOCK

    def to_blocks(a):
        return jnp.swapaxes(a.reshape((B, nb, QBLOCK) + a.shape[2:]), 0, 1)

    ikf = ik.astype(jnp.float32)
    key_pos = jnp.arange(S)

    def one_block(args):
        qb, iqb, iwb, n = args
        tpos = n * QBLOCK + jnp.arange(QBLOCK)
        s = jax.nn.relu(jnp.einsum('bqhd,bsd->bqhs', iqb.astype(jnp.float32), ikf))
        score = jnp.einsum('bqhs,bqh->bqs', s, iwb.astype(jnp.float32))
        causal = key_pos[None, :] <= tpos[:, None]
        score = jnp.where(causal[None], score, -jnp.inf)
        _, idx = lax.top_k(score, topk)
        kg = jax.vmap(lambda kk, ii: kk[ii])(k, idx)
        vg = jax.vmap(lambda vv, ii: vv[ii])(v, idx)
        logits = jnp.einsum('bqhd,bqkhd->bhqk', qb, kg).astype(jnp.float32)
        valid = idx <= tpos[None, :, None]
        logits = jnp.where(valid[:, None], logits, -jnp.inf)
        p = jax.nn.softmax(logits, axis=-1)
        return jnp.einsum('bhqk,bqkhd->bqhd', p.astype(v.dtype), vg)

    out = lax.map(one_block, (to_blocks(q), to_blocks(iq), to_blocks(iw),
                              jnp.arange(nb, dtype=jnp.int32)))
    return jnp.swapaxes(out, 0, 1).reshape(B, S, H, D)


def pool_mixer(u, w_pool, scale):
    B, S, _ = u.shape
    ug = u.astype(jnp.float32).reshape(B, S, POOL_GROUPS, POOL_GROUP_DIM)
    c = jnp.concatenate([jnp.zeros((B, 1, POOL_GROUPS, POOL_GROUP_DIM), jnp.float32),
                         jnp.cumsum(ug, axis=1)], axis=1)
    t = jnp.arange(S)
    pooled = []
    for g, w in enumerate(POOL_WINDOWS):
        start = jnp.maximum(t + 1 - w, 0)
        count = jnp.minimum(t + 1, w).astype(jnp.float32)
        mean = (c[:, t + 1, g] - c[:, start, g]) / count[None, :, None]
        pooled.append(mean - ug[:, :, g])
    pooled = jnp.stack(pooled, axis=2)
    y = jnp.einsum('bsgc,gcd->bsgd', pooled, w_pool.astype(jnp.float32))
    y = y * scale.astype(jnp.float32).reshape(POOL_GROUPS, POOL_GROUP_DIM)
    return y.reshape(B, S, POOL_WIDTH).astype(u.dtype)


def gla_chunked(q, k, v, glog):
    B, S, H, DK = q.shape
    DV = v.shape[-1]
    C = GLA_CHUNK
    N = S // C

    def to_chunks(a):
        return a.astype(jnp.float32).reshape(B, N, C, H, a.shape[-1]).transpose(1, 0, 3, 2, 4)

    qc, kc, vc, gc = to_chunks(q), to_chunks(k), to_chunks(v), to_chunks(glog)
    b = jnp.cumsum(gc, axis=-2)
    qt = qc * jnp.exp(b) * (DK ** -0.5)
    kt = kc * jnp.exp(-b)
    kd = kc * jnp.exp(b[..., -1:, :] - b)
    decay_end = jnp.exp(b[..., -1, :])
    tri = jnp.tril(jnp.ones((C, C), dtype=bool))
    a_intra = jnp.where(tri, jnp.einsum('nbhid,nbhjd->nbhij', qt, kt), 0.0)
    o_intra = jnp.einsum('nbhij,nbhje->nbhie', a_intra, vc)

    def step(state, inp):
        qt_n, kd_n, v_n, dec_n = inp
        o = jnp.einsum('bhid,bhde->bhie', qt_n, state)
        state = dec_n[..., None] * state + jnp.einsum('bhjd,bhje->bhde', kd_n, v_n)
        return state, o

    state0 = jnp.zeros((B, H, DK, DV), jnp.float32)
    _, o_inter = lax.scan(step, state0, (qt, kd, vc, decay_end))
    o = (o_intra + o_inter).transpose(1, 0, 3, 2, 4).reshape(B, S, H, DV)
    return o.astype(v.dtype)


def hybrid_layer(x, norm_g, w_in, att_q_gain, att_k_gain, pool_w, pool_scale,
                 gla_w_gate, gla_b_gate, gla_out_gain, w_out):
    B, S, _ = x.shape
    h = rms_norm(x, norm_g)
    proj = h @ w_in
    (aq, ak, av, az, iq, ik, iw, pu, pz, gq, gk, gv, gg, gz) = split_columns(proj)

    q = rms_norm(aq.reshape(B, S, ATT_HEADS, ATT_HEAD_DIM), att_q_gain) * (ATT_HEAD_DIM ** -0.5)
    k = rms_norm(ak.reshape(B, S, ATT_HEADS, ATT_HEAD_DIM), att_k_gain)
    v = av.reshape(B, S, ATT_HEADS, ATT_HEAD_DIM)
    iq = iq.reshape(B, S, IDX_HEADS, IDX_DIM)
    iw = iw * (IDX_HEADS ** -0.5)
    ya = sparse_attention(q, k, v, iq, ik, iw).reshape(B, S, ATT_WIDTH) * jax.nn.silu(az)

    yb = pool_mixer(pu, pool_w, pool_scale) * jax.nn.silu(pz)

    glog = jax.nn.log_sigmoid((gg @ gla_w_gate + gla_b_gate).astype(jnp.float32)) / GLA_TAU
    glog = glog.reshape(B, S, GLA_HEADS, GLA_DK)
    yc = gla_chunked(gq.reshape(B, S, GLA_HEADS, GLA_DK), gk.reshape(B, S, GLA_HEADS, GLA_DK),
                     gv.reshape(B, S, GLA_HEADS, GLA_DV), glog)
    yc = rms_norm(yc, gla_out_gain).reshape(B, S, GLA_VAL_WIDTH) * jax.nn.silu(gz)

    y = jnp.concatenate([ya, yb, yc], axis=-1) @ w_out
    return x + y


def setup_inputs(seed: int = 0) -> dict:
    key = jax.random.key(seed)
    ks = jax.random.split(key, 12)
    f32 = jnp.float32
    x = jax.random.normal(ks[0], (BATCH, SEQ, D_MODEL), f32)
    norm_g = 1.0 + 0.05 * jax.random.normal(ks[1], (DEPTH, D_MODEL), f32)
    w_in = jax.random.normal(ks[2], (DEPTH, D_MODEL, IN_WIDTH), f32) * D_MODEL ** -0.5
    att_q_gain = 1.0 + 0.05 * jax.random.normal(ks[3], (DEPTH, ATT_HEAD_DIM), f32)
    att_k_gain = 1.0 + 0.05 * jax.random.normal(ks[4], (DEPTH, ATT_HEAD_DIM), f32)
    pool_w = jax.random.normal(ks[5], (DEPTH, POOL_GROUPS, POOL_GROUP_DIM, POOL_GROUP_DIM), f32) * POOL_GROUP_DIM ** -0.5
    pool_scale = 1.0 + 0.1 * jax.random.normal(ks[6], (DEPTH, POOL_WIDTH), f32)
    gla_w_gate = jax.random.normal(ks[7], (DEPTH, GLA_GATE_RANK, GLA_KEY_WIDTH), f32) * GLA_GATE_RANK ** -0.5
    gla_b_gate = 0.1 * jax.random.normal(ks[8], (DEPTH, GLA_KEY_WIDTH), f32)
    gla_out_gain = 1.0 + 0.05 * jax.random.normal(ks[9], (DEPTH, GLA_DV), f32)
    w_out = jax.random.normal(ks[10], (DEPTH, MIX_WIDTH, D_MODEL), f32) * MIX_WIDTH ** -0.5
    return {"x": x, "norm_g": norm_g, "w_in": w_in, "att_q_gain": att_q_gain,
            "att_k_gain": att_k_gain, "pool_w": pool_w, "pool_scale": pool_scale,
            "gla_w_gate": gla_w_gate, "gla_b_gate": gla_b_gate,
            "gla_out_gain": gla_out_gain, "w_out": w_out}


def reference(x, norm_g, w_in, att_q_gain, att_k_gain, pool_w, pool_scale,
              gla_w_gate, gla_b_gate, gla_out_gain, w_out):
    for l in range(DEPTH):
        x = hybrid_layer(x, norm_g[l], w_in[l], att_q_gain[l], att_k_gain[l], pool_w[l],
                         pool_scale[l], gla_w_gate[l], gla_b_gate[l], gla_out_gain[l], w_out[l])
    return x
```

```python
import functools

import jax
import jax.numpy as jnp
from jax import lax
from jax.experimental import pallas as pl
from jax.experimental.pallas import tpu as pltpu

F32 = jnp.float32
BF16 = jnp.bfloat16
I32 = jnp.int32

D_MODEL = 1024
ATT_HEADS = 6
ATT_HEAD_DIM = 64
ATT_WIDTH = ATT_HEADS * ATT_HEAD_DIM
IDX_HEADS = 4
IDX_DIM = 64
TOPK_MAX = 256
POOL_GROUPS = 4
POOL_GROUP_DIM = 64
POOL_WIDTH = POOL_GROUPS * POOL_GROUP_DIM
POOL_WINDOWS = (2, 4, 8, 16)
GLA_HEADS = 4
GLA_DK = 48
GLA_DV = 96
GLA_KEY_WIDTH = GLA_HEADS * GLA_DK
GLA_VAL_WIDTH = GLA_HEADS * GLA_DV
GLA_GATE_RANK = 16
GLA_TAU = 16.0
GLA_CHUNK = 64
EPS = 1e-6

IN_SPLITS = (
    ATT_WIDTH, ATT_WIDTH, ATT_WIDTH, ATT_WIDTH,
    IDX_HEADS * IDX_DIM, IDX_DIM, IDX_HEADS,
    POOL_WIDTH, POOL_WIDTH,
    GLA_KEY_WIDTH, GLA_KEY_WIDTH, GLA_VAL_WIDTH,
    GLA_GATE_RANK, GLA_VAL_WIDTH,
)

GLA_DK_PAD = 64
GLA_DV_PAD = 128
GLA_KEY_PAD = GLA_HEADS * GLA_DK_PAD
GLA_VAL_PAD = GLA_HEADS * GLA_DV_PAD
GATE_PAD = 128

_NAT_PIECES = (
    ("q", ATT_WIDTH), ("k", ATT_WIDTH), ("z", ATT_WIDTH),
    ("iq", IDX_HEADS * IDX_DIM), ("ik2", 2 * IDX_DIM),
    ("pu", POOL_WIDTH), ("pz", POOL_WIDTH),
    ("gq", GLA_KEY_PAD), ("gk", GLA_KEY_PAD), ("gv", GLA_VAL_PAD),
    ("gz", GLA_VAL_PAD), ("gg", GATE_PAD),
)
_NAT_OFF = {}
_off = 0
for _name, _w in _NAT_PIECES:
    _NAT_OFF[_name] = (_off, _off + _w)
    _off += _w
NAT_WIDTH = _off
VT_ROWS = ATT_WIDTH
IW_ROWS = 8
TRN_ROWS = VT_ROWS + IW_ROWS

INT_MIN = -(2 ** 31)
NEG_BIG = -1e30
VMEM_LIMIT = 48 * 1024 * 1024

_NT = (((1,), (1,)), ((), ()))
_TN = (((0,), (0,)), ((), ()))


def _silu(x):
    return x * jax.nn.sigmoid(x)


def _proj_kernel(x_ref, ng_ref, wn_ref, wt_ref, qg_ref, kg_ref, gm_ref,
                 q_ref, k_ref, gaz_ref, iq_ref, ik_ref, pu_ref, gpz_ref,
                 gq_ref, gk_ref, gv_ref, ggz_ref, gg_ref, vt_ref, iwt_ref, *, kc):
    x = x_ref[...]
    ms = jnp.mean(x * x, axis=-1, keepdims=True)
    h = (x * lax.rsqrt(ms + EPS) * ng_ref[...]).astype(BF16)

    def nat(name):
        a, b = _NAT_OFF[name]
        return jnp.dot(h, wn_ref[:, a:b], preferred_element_type=F32)

    def head_norm(a, gain):
        sq = a * a
        hi = sq.astype(BF16)
        lo = (sq - hi.astype(F32)).astype(BF16)
        gm = gm_ref[...]
        hms = (jnp.dot(hi, gm, preferred_element_type=F32)
               + jnp.dot(lo, gm, preferred_element_type=F32))
        return a * lax.rsqrt(hms + EPS) * gain

    q_ref[...] = head_norm(nat("q"), qg_ref[...]).astype(BF16)
    k_ref[...] = head_norm(nat("k"), kg_ref[...]).astype(BF16)
    gaz_ref[...] = _silu(nat("z"))
    iq_ref[...] = nat("iq").astype(BF16)
    ik_ref[...] = nat("ik2").astype(BF16)
    pu_ref[...] = nat("pu")
    gpz_ref[...] = _silu(nat("pz"))
    gq_ref[...] = nat("gq")
    gk_ref[...] = nat("gk")
    gv_ref[...] = nat("gv").astype(BF16)
    ggz_ref[...] = _silu(nat("gz"))
    gg_ref[...] = nat("gg")

    pt = lax.dot_general(wt_ref[...], h, _NT, preferred_element_type=F32)
    tm = x.shape[0]
    for a in range(tm // kc):
        vt_ref[a] = pt[0:VT_ROWS, a * kc:(a + 1) * kc].astype(BF16)
    iwt_ref[...] = pt[VT_ROWS:TRN_ROWS, :] * (IDX_HEADS ** -0.5)


def _proj_call(x2, ng, wn, wt, qg, kg, gm, *, tm, kc):
    T = x2.shape[0]
    grid = (T // tm,)
    row = lambda w: pl.BlockSpec((tm, w), lambda i: (i, 0))
    full = lambda a: pl.BlockSpec(a.shape, lambda i: (0,) * a.ndim)
    out_shape = (
        jax.ShapeDtypeStruct((T, ATT_WIDTH), BF16),
        jax.ShapeDtypeStruct((T, ATT_WIDTH), BF16),
        jax.ShapeDtypeStruct((T, ATT_WIDTH), F32),
        jax.ShapeDtypeStruct((T, IDX_HEADS * IDX_DIM), BF16),
        jax.ShapeDtypeStruct((T, 2 * IDX_DIM), BF16),
        jax.ShapeDtypeStruct((T, POOL_WIDTH), F32),
        jax.ShapeDtypeStruct((T, POOL_WIDTH), F32),
        jax.ShapeDtypeStruct((T, GLA_KEY_PAD), F32),
        jax.ShapeDtypeStruct((T, GLA_KEY_PAD), F32),
        jax.ShapeDtypeStruct((T, GLA_VAL_PAD), BF16),
        jax.ShapeDtypeStruct((T, GLA_VAL_PAD), F32),
        jax.ShapeDtypeStruct((T, GATE_PAD), F32),
        jax.ShapeDtypeStruct((T // kc, VT_ROWS, kc), BF16),
        jax.ShapeDtypeStruct((IW_ROWS, T), F32),
    )
    out_specs = (
        row(ATT_WIDTH), row(ATT_WIDTH), row(ATT_WIDTH),
        row(IDX_HEADS * IDX_DIM), row(2 * IDX_DIM),
        row(POOL_WIDTH), row(POOL_WIDTH),
        row(GLA_KEY_PAD), row(GLA_KEY_PAD), row(GLA_VAL_PAD), row(GLA_VAL_PAD),
        row(GATE_PAD),
        pl.BlockSpec((tm // kc, VT_ROWS, kc), lambda i: (i, 0, 0)),
        pl.BlockSpec((IW_ROWS, tm), lambda i: (0, i)),
    )
    return pl.pallas_call(
        functools.partial(_proj_kernel, kc=kc),
        grid=grid,
        in_specs=[row(D_MODEL), full(ng), full(wn), full(wt), full(qg), full(kg), full(gm)],
        out_specs=out_specs,
        out_shape=out_shape,
        compiler_params=pltpu.CompilerParams(
            dimension_semantics=("parallel",), vmem_limit_bytes=VMEM_LIMIT),
        name="proj",
    )(x2, ng, wn, wt, qg, kg, gm)


def _attn_kernel(q_ref, k_ref, vt_ref, iq_ref, ik_ref, iwt_ref, gaz_ref,
                 ya_ref, ukey_ref, bias_ref, ot_ref, *, tq, topk):
    j = pl.program_id(1)
    nch = j + 1
    lane = lax.broadcasted_iota(I32, (tq, 128), 1)
    lo_half = lane < 64
    row_i = lax.broadcasted_iota(I32, (tq, tq), 0)
    col_i = lax.broadcasted_iota(I32, (tq, tq), 1)
    t_idx = j * tq + col_i

    def half_mask(slab, h):
        keep = lo_half if h % 2 == 0 else jnp.logical_not(lo_half)
        return jnp.where(keep, slab, jnp.zeros_like(slab))

    iqm = [half_mask(iq_ref[:, 128 * (h // 2):128 * (h // 2) + 128], h) for h in range(IDX_HEADS)]
    iw = iwt_ref[...]

    def key_body(c, carry):
        r0 = pl.multiple_of(c * tq, tq)
        ikc = ik_ref[pl.ds(r0, tq), :]
        acc = jnp.zeros((tq, tq), F32)
        for h in range(IDX_HEADS):
            d = lax.dot_general(ikc, iqm[h], _NT, preferred_element_type=F32)
            acc = acc + jnp.maximum(d, 0.0) * iw[h:h + 1, :]
        bits = lax.bitcast_convert_type(acc, I32)
        key = bits ^ ((bits >> 31) & 0x7FFFFFFF)
        key = jnp.where(r0 + row_i <= t_idx, key, INT_MIN)
        ukey_ref[pl.ds(r0, tq), :] = key
        return carry

    lax.fori_loop(0, nch, key_body, 0)

    def count(pred):
        def body(c, cnt):
            r0 = pl.multiple_of(c * tq, tq)
            u = ukey_ref[pl.ds(r0, tq), :]
            ones = jnp.where(pred(u), 1, 0).astype(I32)
            return cnt + ones.reshape(tq // 8, 8, tq).sum(axis=0)
        cnt8 = lax.fori_loop(0, nch, body, jnp.zeros((8, tq), I32))
        return cnt8.sum(axis=0, keepdims=True)

    c0 = count(lambda u: u >= 0)
    cur0 = jnp.where(c0 >= topk, 0, INT_MIN).astype(I32)

    def bis_body(i, cur):
        cand = cur + jnp.left_shift(jnp.int32(1), 30 - i)
        cnt = count(lambda u: u >= cand)
        return jnp.where(cnt >= topk, cand, cur)

    thr = lax.fori_loop(0, 31, bis_body, cur0)
    n_gt = count(lambda u: u > thr)
    need = jnp.where(thr == INT_MIN, 0, topk - n_gt).astype(F32)
    tri = jnp.where(row_i >= col_i, 1.0, 0.0).astype(BF16)

    def bias_body(c, run):
        r0 = pl.multiple_of(c * tq, tq)
        u = ukey_ref[pl.ds(r0, tq), :]
        tie = u == thr
        tie_f = jnp.where(tie, 1.0, 0.0).astype(BF16)
        rank = run + jnp.dot(tri, tie_f, preferred_element_type=F32)
        take_tie = jnp.where(tie, jnp.where(rank <= need, 0.0, NEG_BIG), NEG_BIG)
        bias_ref[pl.ds(r0, tq), :] = jnp.where(u > thr, 0.0, take_tie)
        return rank[tq - 1:tq, :]

    lax.fori_loop(0, nch, bias_body, jnp.zeros((1, tq), F32))

    for h in range(ATT_HEADS):
        p = h // 2
        qm = half_mask(q_ref[:, 128 * p:128 * p + 128], h)

        def att_body(c, carry, h=h, p=p, qm=qm):
            m, l, acc = carry
            r0 = pl.multiple_of(c * tq, tq)
            kch = k_ref[pl.ds(r0, tq), 128 * p:128 * p + 128]
            lg = lax.dot_general(kch, qm, _NT, preferred_element_type=F32)
            lg = lg + bias_ref[pl.ds(r0, tq), :]
            m_new = jnp.maximum(m, lg.max(axis=0, keepdims=True))
            alpha = jnp.exp(m - m_new)
            pr = jnp.exp(lg - m_new)
            l = alpha * l + pr.sum(axis=0, keepdims=True)
            vth = vt_ref[c, 64 * h:64 * h + 64, :]
            acc = alpha * acc + jnp.dot(vth, pr.astype(BF16), preferred_element_type=F32)
            return m_new, l, acc

        m0 = jnp.full((1, tq), NEG_BIG, F32)
        l0 = jnp.zeros((1, tq), F32)
        a0 = jnp.zeros((ATT_HEAD_DIM, tq), F32)
        _, l, acc = lax.fori_loop(0, nch, att_body, (m0, l0, a0))
        ot_ref[64 * h:64 * h + 64, :] = acc / l

    ya_ref[...] = (ot_ref[...].T * gaz_ref[...]).astype(BF16)


def _attn_call(q, k, vt, iq, ik, iwt, gaz, *, tq, topk):
    B, S, _ = q.shape
    nq = S // tq
    qtile = lambda w: pl.BlockSpec((None, tq, w), lambda b, j: (b, j, 0))
    seq = lambda w: pl.BlockSpec((None, S, w), lambda b, j: (b, 0, 0))
    return pl.pallas_call(
        functools.partial(_attn_kernel, tq=tq, topk=topk),
        grid=(B, nq),
        in_specs=[
            qtile(ATT_WIDTH), seq(ATT_WIDTH),
            pl.BlockSpec((None, nq, VT_ROWS, tq), lambda b, j: (b, 0, 0, 0)),
            qtile(IDX_HEADS * IDX_DIM), seq(2 * IDX_DIM),
            pl.BlockSpec((None, IW_ROWS, tq), lambda b, j: (b, 0, j)),
            qtile(ATT_WIDTH),
        ],
        out_specs=qtile(ATT_WIDTH),
        out_shape=jax.ShapeDtypeStruct((B, S, ATT_WIDTH), BF16),
        scratch_shapes=[
            pltpu.VMEM((S, tq), I32),
            pltpu.VMEM((S, tq), F32),
            pltpu.VMEM((ATT_WIDTH, tq), F32),
        ],
        compiler_params=pltpu.CompilerParams(
            dimension_semantics=("parallel", "arbitrary"), vmem_limit_bytes=VMEM_LIMIT),
        name="attn",
    )(q, k, vt, iq, ik, iwt, gaz)


def _mix_kernel(pu_ref, gpz_ref, gq_ref, gk_ref, gv_ref, ggz_ref, gg_ref,
                pw_ref, ps_ref, wg_ref, bg_ref, og_ref,
                yb_ref, yc_ref, pad_ref, st_ref, *, seq, rb):
    wmax = max(POOL_WINDOWS)
    pad_ref[0:wmax, :] = jnp.zeros((wmax, POOL_WIDTH), F32)
    pad_ref[wmax:wmax + seq, :] = pu_ref[...]
    lane = lax.broadcasted_iota(I32, (rb, POOL_WIDTH), 1)
    grp = lane // POOL_GROUP_DIM
    win = jnp.zeros((rb, POOL_WIDTH), I32)
    for g, w in enumerate(POOL_WINDOWS):
        win = jnp.where(grp == g, w, win)
    for blk in range(seq // rb):
        r0 = blk * rb
        run = jnp.zeros((rb, POOL_WIDTH), F32)
        wsum = jnp.zeros((rb, POOL_WIDTH), F32)
        for jj in range(wmax):
            run = run + pad_ref[wmax + r0 - jj:wmax + r0 - jj + rb, :]
            if (jj + 1) in POOL_WINDOWS:
                wsum = jnp.where(win == jj + 1, run, wsum)
        t1 = r0 + 1 + lax.broadcasted_iota(I32, (rb, POOL_WIDTH), 0)
        cnt = jnp.minimum(t1, win).astype(F32)
        pooled = wsum / cnt - pad_ref[wmax + r0:wmax + r0 + rb, :]
        y = jnp.dot(pooled.astype(BF16), pw_ref[...], preferred_element_type=F32)
        y = y * ps_ref[...] * gpz_ref[r0:r0 + rb, :]
        yb_ref[r0:r0 + rb, :] = y.astype(BF16)

    C = GLA_CHUNK
    H = GLA_HEADS
    ri = lax.broadcasted_iota(I32, (C, C), 0)
    ci = lax.broadcasted_iota(I32, (C, C), 1)
    tri = jnp.where(ri >= ci, 1.0, 0.0).astype(F32)
    klane = lax.broadcasted_iota(I32, (C, GLA_KEY_PAD), 1) // GLA_DK_PAD
    slane = lax.broadcasted_iota(I32, (GLA_DV_PAD, GLA_KEY_PAD), 1) // GLA_DK_PAD
    st_ref[...] = jnp.zeros((GLA_DV_PAD, GLA_KEY_PAD), F32)

    def chunk_body(n, carry):
        r0 = pl.multiple_of(n * C, C)
        z = jnp.dot(gg_ref[pl.ds(r0, C), :], wg_ref[...], preferred_element_type=F32,
                    precision=lax.Precision.HIGHEST) + bg_ref[...]
        g = jax.nn.log_sigmoid(z) * (1.0 / GLA_TAU)
        b = jnp.dot(tri, g, preferred_element_type=F32, precision=lax.Precision.HIGHEST)
        bend = b[C - 1:C, :]
        qc = gq_ref[pl.ds(r0, C), :]
        kcv = gk_ref[pl.ds(r0, C), :]
        v = gv_ref[pl.ds(r0, C), :]
        qt = qc * jnp.exp(b) * (GLA_DK ** -0.5)
        kt = (kcv * jnp.exp(-b)).astype(BF16)
        kd = (kcv * jnp.exp(bend - b)).astype(BF16)
        dec = jnp.exp(bend)
        qs = jnp.concatenate(
            [jnp.where(klane == h, qt, 0.0) for h in range(H)], axis=0).astype(BF16)
        st = st_ref[...]
        a = lax.dot_general(qs, kt, _NT, preferred_element_type=F32)
        o_inter = lax.dot_general(qs, st.astype(BF16), _NT, preferred_element_type=F32)
        for h in range(H):
            ah = jnp.where(ri >= ci, a[h * C:(h + 1) * C, :], 0.0).astype(BF16)
            vh = v[:, h * GLA_DV_PAD:(h + 1) * GLA_DV_PAD]
            o = jnp.dot(ah, vh, preferred_element_type=F32) + o_inter[h * C:(h + 1) * C, :]
            ms = jnp.sum(o * o, axis=-1, keepdims=True) * (1.0 / GLA_DV)
            y = o * lax.rsqrt(ms + EPS) * og_ref[...]
            y = y * ggz_ref[pl.ds(r0, C), h * GLA_DV_PAD:(h + 1) * GLA_DV_PAD]
            yc_ref[pl.ds(r0, C), h * GLA_DV_PAD:(h + 1) * GLA_DV_PAD] = y.astype(BF16)
        upd = lax.dot_general(v, kd, _TN, preferred_element_type=F32)
        new = dec * st
        for h in range(H):
            new = new + jnp.where(slane == h, upd[h * GLA_DV_PAD:(h + 1) * GLA_DV_PAD, :], 0.0)
        st_ref[...] = new
        return carry

    lax.fori_loop(0, seq // C, chunk_body, 0)


def _mix_call(pu, gpz, gq, gk, gv, ggz, gg, pw, ps, wg, bg, og):
    B, S, _ = pu.shape
    rb = min(256, S)
    seqb = lambda w: pl.BlockSpec((None, S, w), lambda b: (b, 0, 0))
    full = lambda a: pl.BlockSpec(a.shape, lambda b: (0,) * a.ndim)
    return pl.pallas_call(
        functools.partial(_mix_kernel, seq=S, rb=rb),
        grid=(B,),
        in_specs=[seqb(POOL_WIDTH), seqb(POOL_WIDTH), seqb(GLA_KEY_PAD), seqb(GLA_KEY_PAD),
                  seqb(GLA_VAL_PAD), seqb(GLA_VAL_PAD), seqb(GATE_PAD),
                  full(pw), full(ps), full(wg), full(bg), full(og)],
        out_specs=(seqb(POOL_WIDTH), seqb(GLA_VAL_PAD)),
        out_shape=(jax.ShapeDtypeStruct((B, S, POOL_WIDTH), BF16),
                   jax.ShapeDtypeStruct((B, S, GLA_VAL_PAD), BF16)),
        scratch_shapes=[
            pltpu.VMEM((max(POOL_WINDOWS) + S, POOL_WIDTH), F32),
            pltpu.VMEM((GLA_DV_PAD, GLA_KEY_PAD), F32),
        ],
        compiler_params=pltpu.CompilerParams(
            dimension_semantics=("parallel",), vmem_limit_bytes=VMEM_LIMIT),
        name="mix",
    )(pu, gpz, gq, gk, gv, ggz, gg, pw, ps, wg, bg, og)


def _out_kernel(x_ref, ya_ref, yb_ref, yc_ref, wa_ref, wb_ref, wc_ref, o_ref):
    y = jnp.dot(ya_ref[...], wa_ref[...], preferred_element_type=F32)
    y = y + jnp.dot(yb_ref[...], wb_ref[...], preferred_element_type=F32)
    y = y + jnp.dot(yc_ref[...], wc_ref[...], preferred_element_type=F32)
    o_ref[...] = x_ref[...] + y


def _out_call(x2, ya, yb, yc, wa, wb, wc, *, tm):
    T = x2.shape[0]
    row = lambda w: pl.BlockSpec((tm, w), lambda i: (i, 0))
    full = lambda a: pl.BlockSpec(a.shape, lambda i: (0,) * a.ndim)
    return pl.pallas_call(
        _out_kernel,
        grid=(T // tm,),
        in_specs=[row(D_MODEL), row(ATT_WIDTH), row(POOL_WIDTH), row(GLA_VAL_PAD),
                  full(wa), full(wb), full(wc)],
        out_specs=row(D_MODEL),
        out_shape=jax.ShapeDtypeStruct((T, D_MODEL), F32),
        compiler_params=pltpu.CompilerParams(
            dimension_semantics=("parallel",), vmem_limit_bytes=VMEM_LIMIT),
        name="outproj",
    )(x2, ya, yb, yc, wa, wb, wc)


def _pad_heads(w, heads, d, dpad, axis):
    shp = list(w.shape)
    w = w.reshape(shp[:axis] + [heads, d] + shp[axis + 1:])
    pad = [(0, 0)] * w.ndim
    pad[axis + 1] = (0, dpad - d)
    w = jnp.pad(w, pad)
    return w.reshape(shp[:axis] + [heads * dpad] + shp[axis + 1:])


def _pack_layer(norm_g, w_in, att_q_gain, att_k_gain, pool_w, pool_scale,
                gla_w_gate, gla_b_gate, gla_out_gain, w_out):
    offs = [0]
    for s in IN_SPLITS:
        offs.append(offs[-1] + s)
    (aq, ak, av, az, iq, ik, iw, pu, pz, gq, gk, gv, gg, gz) = [
        w_in[:, offs[i]:offs[i + 1]] for i in range(len(IN_SPLITS))]
    nat = {
        "q": aq, "k": ak, "z": az, "iq": iq,
        "ik2": jnp.concatenate([ik, ik], axis=1),
        "pu": pu, "pz": pz,
        "gq": _pad_heads(gq, GLA_HEADS, GLA_DK, GLA_DK_PAD, 1),
        "gk": _pad_heads(gk, GLA_HEADS, GLA_DK, GLA_DK_PAD, 1),
        "gv": _pad_heads(gv, GLA_HEADS, GLA_DV, GLA_DV_PAD, 1),
        "gz": _pad_heads(gz, GLA_HEADS, GLA_DV, GLA_DV_PAD, 1),
        "gg": jnp.pad(gg, ((0, 0), (0, GATE_PAD - GLA_GATE_RANK))),
    }
    wn = jnp.concatenate([nat[name] for name, _ in _NAT_PIECES], axis=1).astype(BF16)
    wt = jnp.concatenate(
        [av.T, jnp.pad(iw.T, ((0, IW_ROWS - IDX_HEADS), (0, 0)))], axis=0).astype(BF16)
    qg = (jnp.tile(att_q_gain, ATT_HEADS) * (ATT_HEAD_DIM ** -0.5)).reshape(1, ATT_WIDTH)
    kg = jnp.tile(att_k_gain, ATT_HEADS).reshape(1, ATT_WIDTH)
    hid = jnp.arange(ATT_WIDTH) // ATT_HEAD_DIM
    gm = jnp.where(hid[:, None] == hid[None, :], 1.0 / ATT_HEAD_DIM, 0.0).astype(BF16)
    pw = jax.scipy.linalg.block_diag(*[pool_w[g] for g in range(POOL_GROUPS)]).astype(BF16)
    ps = pool_scale.reshape(1, POOL_WIDTH)
    wg = jnp.pad(_pad_heads(gla_w_gate, GLA_HEADS, GLA_DK, GLA_DK_PAD, 1),
                 ((0, GATE_PAD - GLA_GATE_RANK), (0, 0)))
    bg = _pad_heads(gla_b_gate.reshape(1, -1), GLA_HEADS, GLA_DK, GLA_DK_PAD, 1)
    og = jnp.pad(gla_out_gain, (0, GLA_DV_PAD - GLA_DV)).reshape(1, GLA_DV_PAD)
    wa = w_out[0:ATT_WIDTH].astype(BF16)
    wb = w_out[ATT_WIDTH:ATT_WIDTH + POOL_WIDTH].astype(BF16)
    wc = _pad_heads(w_out[ATT_WIDTH + POOL_WIDTH:], GLA_HEADS, GLA_DV, GLA_DV_PAD, 0).astype(BF16)
    return dict(ng=norm_g.reshape(1, D_MODEL), wn=wn, wt=wt, qg=qg, kg=kg, gm=gm,
                pw=pw, ps=ps, wg=wg, bg=bg, og=og, wa=wa, wb=wb, wc=wc)


def _layer(x2, p, *, B, S, tm, tq, topk):
    (q, k, gaz, iq, ik, pu, gpz, gq, gk, gv, ggz, gg, vt, iwt) = _proj_call(
        x2, p["ng"], p["wn"], p["wt"], p["qg"], p["kg"], p["gm"], tm=tm, kc=tq)
    r3 = lambda a: a.reshape(B, S, a.shape[-1])
    ya = _attn_call(r3(q), r3(k), vt.reshape(B, S // tq, VT_ROWS, tq), r3(iq), r3(ik),
                    iwt.reshape(IW_ROWS, B, S).transpose(1, 0, 2), r3(gaz), tq=tq, topk=topk)
    yb, yc = _mix_call(r3(pu), r3(gpz), r3(gq), r3(gk), r3(gv), r3(ggz), r3(gg),
                       p["pw"], p["ps"], p["wg"], p["bg"], p["og"])
    T = B * S
    return _out_call(x2, ya.reshape(T, ATT_WIDTH), yb.reshape(T, POOL_WIDTH),
                     yc.reshape(T, GLA_VAL_PAD), p["wa"], p["wb"], p["wc"], tm=tm)


def _tiles(S):
    tq = min(256, S // 2)
    tm = min(512, S)
    return tm, tq


def kernel(x, norm_g, w_in, att_q_gain, att_k_gain, pool_w, pool_scale, gla_w_gate,
           gla_b_gate, gla_out_gain, w_out):
    B, S, _ = x.shape
    depth = norm_g.shape[0]
    tm, tq = _tiles(S)
    topk = min(TOPK_MAX, S // 4)
    x2 = x.reshape(B * S, D_MODEL)
    for l in range(depth):
        p = _pack_layer(norm_g[l], w_in[l], att_q_gain[l], att_k_gain[l], pool_w[l],
                        pool_scale[l], gla_w_gate[l], gla_b_gate[l], gla_out_gain[l], w_out[l])
        x2 = _layer(x2, p, B=B, S=S, tm=tm, tq=tq, topk=topk)
    return x2.reshape(B, S, D_MODEL)
```

```python
import functools

import jax
import jax.numpy as jnp
from jax import lax
from jax.experimental import pallas as pl
from jax.experimental.pallas import tpu as pltpu

F32 = jnp.float32
BF16 = jnp.bfloat16
I32 = jnp.int32
I16 = jnp.int16

D_MODEL = 1024
ATT_HEADS = 6
ATT_HEAD_DIM = 64
ATT_WIDTH = ATT_HEADS * ATT_HEAD_DIM
IDX_HEADS = 4
IDX_DIM = 64
TOPK_MAX = 256
POOL_GROUPS = 4
POOL_GROUP_DIM = 64
POOL_WIDTH = POOL_GROUPS * POOL_GROUP_DIM
POOL_WINDOWS = (2, 4, 8, 16)
GLA_HEADS = 4
GLA_DK = 48
GLA_DV = 96
GLA_KEY_WIDTH = GLA_HEADS * GLA_DK
GLA_VAL_WIDTH = GLA_HEADS * GLA_DV
GLA_GATE_RANK = 16
GLA_TAU = 16.0
GLA_CHUNK = 64
EPS = 1e-6

IN_SPLITS = (
    ATT_WIDTH, ATT_WIDTH, ATT_WIDTH, ATT_WIDTH,
    IDX_HEADS * IDX_DIM, IDX_DIM, IDX_HEADS,
    POOL_WIDTH, POOL_WIDTH,
    GLA_KEY_WIDTH, GLA_KEY_WIDTH, GLA_VAL_WIDTH,
    GLA_GATE_RANK, GLA_VAL_WIDTH,
)

LANES = 128
BF16_ROWS = 16
GLA_DK_PAD = 64
GLA_DV_PAD = 128
GLA_KEY_PAD = GLA_HEADS * GLA_DK_PAD
GLA_VAL_PAD = GLA_HEADS * GLA_DV_PAD
GATE_PAD = LANES

_NAT_PIECES = (
    ("q", ATT_WIDTH), ("k", ATT_WIDTH), ("z", ATT_WIDTH),
    ("iq", IDX_HEADS * IDX_DIM), ("ik2", 2 * IDX_DIM),
    ("pu", POOL_WIDTH), ("pz", POOL_WIDTH),
    ("gq", GLA_KEY_PAD), ("gk", GLA_KEY_PAD), ("gv", GLA_VAL_PAD),
    ("gz", GLA_VAL_PAD), ("gg", GATE_PAD),
)
_NAT_OFF = {}
_off = 0
for _name, _w in _NAT_PIECES:
    _NAT_OFF[_name] = (_off, _off + _w)
    _off += _w
NAT_WIDTH = _off
VT_ROWS = ATT_WIDTH
IW_ROWS = 8
TRN_ROWS = VT_ROWS + IW_ROWS
PV_ROWS = ATT_HEAD_DIM + BF16_ROWS

INT_MIN = -(2 ** 31)
I16_MIN = -(2 ** 15)
NEG_BIG = -1e30
VMEM_LIMIT = 48 * 1024 * 1024

_NT = (((1,), (1,)), ((), ()))
_TN = (((0,), (0,)), ((), ()))


def _silu(x):
    return x * jax.nn.sigmoid(x)


def _fold_rows(x, rows, op):
    while x.shape[0] > rows:
        half = x.shape[0] // 2
        x = op(x[:half], x[half:])
    return x


def _proj_kernel(x_ref, ng_ref, wn_ref, wt_ref, qg_ref, kg_ref, gm_ref,
                 q_ref, k_ref, gaz_ref, iq_ref, ik_ref, pu_ref, gpz_ref,
                 gq_ref, gk_ref, gv_ref, ggz_ref, gg_ref, vt_ref, iwt_ref, *, kc):
    x = x_ref[...]
    ms = jnp.mean(x * x, axis=-1, keepdims=True)
    h = (x * lax.rsqrt(ms + EPS) * ng_ref[...]).astype(BF16)

    def nat(name):
        a, b = _NAT_OFF[name]
        return jnp.dot(h, wn_ref[:, a:b], preferred_element_type=F32)

    def head_norm(a, gain):
        sq = a * a
        hi = sq.astype(BF16)
        lo = (sq - hi.astype(F32)).astype(BF16)
        gm = gm_ref[...]
        hms = (jnp.dot(hi, gm, preferred_element_type=F32)
               + jnp.dot(lo, gm, preferred_element_type=F32))
        return a * lax.rsqrt(hms + EPS) * gain

    q_ref[...] = head_norm(nat("q"), qg_ref[...]).astype(BF16)
    k_ref[...] = head_norm(nat("k"), kg_ref[...]).astype(BF16)
    gaz_ref[...] = _silu(nat("z"))
    iq_ref[...] = nat("iq").astype(BF16)
    ik_ref[...] = nat("ik2").astype(BF16)
    pu_ref[...] = nat("pu")
    gpz_ref[...] = _silu(nat("pz"))
    gq_ref[...] = nat("gq")
    gk_ref[...] = nat("gk")
    gv_ref[...] = nat("gv").astype(BF16)
    ggz_ref[...] = _silu(nat("gz"))
    gg_ref[...] = nat("gg")

    pt = lax.dot_general(wt_ref[...], h, _NT, preferred_element_type=F32)
    tm = x.shape[0]
    for a in range(tm // kc):
        vt_ref[a] = pt[0:VT_ROWS, a * kc:(a + 1) * kc].astype(BF16)
    iwt_ref[...] = pt[VT_ROWS:TRN_ROWS, :]


def _proj_call(x2, ng, wn, wt, qg, kg, gm, *, tm, kc):
    T = x2.shape[0]
    grid = (T // tm,)
    row = lambda w: pl.BlockSpec((tm, w), lambda i: (i, 0))
    full = lambda a: pl.BlockSpec(a.shape, lambda i: (0,) * a.ndim)
    widths_dtypes = (
        (ATT_WIDTH, BF16), (ATT_WIDTH, BF16),
        (ATT_WIDTH, F32),
        (IDX_HEADS * IDX_DIM, BF16), (2 * IDX_DIM, BF16),
        (POOL_WIDTH, F32), (POOL_WIDTH, F32),
        (GLA_KEY_PAD, F32), (GLA_KEY_PAD, F32),
        (GLA_VAL_PAD, BF16), (GLA_VAL_PAD, F32),
        (GATE_PAD, F32),
    )
    out_shape = tuple(jax.ShapeDtypeStruct((T, w), dt) for w, dt in widths_dtypes) + (
        jax.ShapeDtypeStruct((T // kc, VT_ROWS, kc), BF16),
        jax.ShapeDtypeStruct((IW_ROWS, T), F32),
    )
    out_specs = tuple(row(w) for w, _ in widths_dtypes) + (
        pl.BlockSpec((tm // kc, VT_ROWS, kc), lambda i: (i, 0, 0)),
        pl.BlockSpec((IW_ROWS, tm), lambda i: (0, i)),
    )
    return pl.pallas_call(
        functools.partial(_proj_kernel, kc=kc),
        grid=grid,
        in_specs=[row(D_MODEL), full(ng), full(wn), full(wt), full(qg), full(kg), full(gm)],
        out_specs=out_specs,
        out_shape=out_shape,
        compiler_params=pltpu.CompilerParams(
            dimension_semantics=("parallel",), vmem_limit_bytes=VMEM_LIMIT),
        name="proj",
    )(x2, ng, wn, wt, qg, kg, gm)


def _attn_kernel(q_ref, k_ref, vt_ref, iq_ref, ik_ref, iwt_ref, gaz_ref, ya_ref,
                 k32_ref, khi_ref, klo_ref, bias_ref, lg_ref, mx_ref, racc_ref, ot_ref,
                 *, tq, topk):
    kc = tq
    j = pl.program_id(1)
    nch = j + 1
    lane = lax.broadcasted_iota(I32, (tq, LANES), 1)
    lo_half = lane < 64
    row_i = lax.broadcasted_iota(I32, (kc, tq), 0)
    col_i = lax.broadcasted_iota(I32, (kc, tq), 1)
    t_idx = j * tq + col_i

    def half_mask(slab, h):
        keep = lo_half if h % 2 == 0 else jnp.logical_not(lo_half)
        return jnp.where(keep, slab, jnp.zeros_like(slab))

    iqm = [half_mask(iq_ref[:, LANES * (h // 2):LANES * (h // 2) + LANES], h)
           for h in range(IDX_HEADS)]
    iw = iwt_ref[...]

    def key_body(c, carry):
        r0 = pl.multiple_of(c * kc, kc)
        ikc = ik_ref[pl.ds(r0, kc), :]
        acc = jnp.zeros((kc, tq), F32)
        for h in range(IDX_HEADS):
            d = lax.dot_general(ikc, iqm[h], _NT, preferred_element_type=F32)
            acc = acc + jnp.maximum(d, 0.0) * iw[h:h + 1, :]
        bits = lax.bitcast_convert_type(acc, I32)
        key = bits ^ ((bits >> 31) & 0x7FFFFFFF)
        key = jnp.where(r0 + row_i <= t_idx, key, INT_MIN)
        k32_ref[c] = key
        khi_ref[c] = (key >> 16).astype(I16)
        klo_ref[c] = ((key & 0xFFFF) + I16_MIN).astype(I16)
        return carry

    lax.fori_loop(0, nch, key_body, 0)

    one16 = jnp.ones((kc, tq), I16)
    zero16 = jnp.zeros((kc, tq), I16)

    def count16(ref, pred):
        def body(c, cnt):
            ones = jnp.where(pred(ref[c]), one16, zero16)
            return cnt + _fold_rows(ones, BF16_ROWS, jnp.add)
        cnt = lax.fori_loop(0, nch, body, jnp.zeros((BF16_ROWS, tq), I16))
        return cnt.astype(I32).sum(axis=0, keepdims=True)

    def bisect16(ref, want):
        c0 = count16(ref, lambda u: u >= jnp.int16(0))
        cur0 = jnp.where(c0 >= want, 0, I16_MIN).astype(I32)

        def body(i, cur):
            cand = cur + jnp.left_shift(jnp.int32(1), 14 - i)
            cand16 = cand.astype(I16)
            cnt = count16(ref, lambda u: u >= cand16)
            return jnp.where(cnt >= want, cand, cur)

        return lax.fori_loop(0, 15, body, cur0)

    thr_hi = bisect16(khi_ref, topk)
    thr_hi16 = thr_hi.astype(I16)
    n_hi_gt = count16(khi_ref, lambda u: u > thr_hi16)

    def group_body(c, carry):
        klo_ref[c] = jnp.where(khi_ref[c] == thr_hi16, klo_ref[c], jnp.int16(I16_MIN))
        return carry

    lax.fori_loop(0, nch, group_body, 0)
    thr_lo = bisect16(klo_ref, topk - n_hi_gt)
    thr_lo16 = thr_lo.astype(I16)
    n_gt = n_hi_gt + count16(klo_ref, lambda u: u > thr_lo16)
    thr = (thr_hi << 16) | ((thr_lo - I16_MIN) & 0xFFFF)
    need = jnp.where(thr == INT_MIN, 0, topk - n_gt).astype(F32)
    tri = jnp.where(row_i >= col_i, 1.0, 0.0).astype(BF16)

    def bias_body(c, run):
        u = k32_ref[c]
        tie = u == thr
        tie_f = jnp.where(tie, 1.0, 0.0).astype(BF16)
        rank = run + jnp.dot(tri, tie_f, preferred_element_type=F32)
        take_tie = jnp.where(tie, jnp.where(rank <= need, 0.0, NEG_BIG), NEG_BIG)
        bias_ref[c] = jnp.where(u > thr, 0.0, take_tie)
        return rank[kc - 1:kc, :]

    lax.fori_loop(0, nch, bias_body, jnp.zeros((1, tq), F32))

    qm = [half_mask(q_ref[:, LANES * (h // 2):LANES * (h // 2) + LANES], h)
          for h in range(ATT_HEADS)]
    mx_ref[...] = jnp.full(mx_ref.shape, NEG_BIG, F32)
    racc_ref[...] = jnp.zeros(racc_ref.shape, F32)

    def logit_body(c, carry):
        r0 = pl.multiple_of(c * kc, kc)
        bias = bias_ref[c]
        for h in range(ATT_HEADS):
            p = h // 2
            kch = k_ref[pl.ds(r0, kc), LANES * p:LANES * p + LANES]
            lg = lax.dot_general(kch, qm[h], _NT, preferred_element_type=F32) + bias
            lg_ref[h, c] = lg
            mx_ref[h] = jnp.maximum(mx_ref[h], _fold_rows(lg, 8, jnp.maximum))
        return carry

    lax.fori_loop(0, nch, logit_body, 0)
    mrow = [jnp.max(mx_ref[h], axis=0, keepdims=True) for h in range(ATT_HEADS)]
    ones_rows = jnp.ones((BF16_ROWS, kc), BF16)

    def pv_body(c, carry):
        for h in range(ATT_HEADS):
            vaug = jnp.concatenate(
                [vt_ref[c, ATT_HEAD_DIM * h:ATT_HEAD_DIM * (h + 1), :], ones_rows], axis=0)
            pr = jnp.exp(lg_ref[h, c] - mrow[h]).astype(BF16)
            racc_ref[h] += jnp.dot(vaug, pr, preferred_element_type=F32)
        return carry

    lax.fori_loop(0, nch, pv_body, 0)

    for h in range(ATT_HEADS):
        r = racc_ref[h]
        ot_ref[ATT_HEAD_DIM * h:ATT_HEAD_DIM * (h + 1), :] = (
            r[0:ATT_HEAD_DIM, :] / r[ATT_HEAD_DIM:ATT_HEAD_DIM + 1, :])
    ya_ref[...] = (ot_ref[...].T * gaz_ref[...]).astype(BF16)


def _attn_call(q, k, vt, iq, ik, iwt, gaz, *, tq, topk):
    B, S, _ = q.shape
    nq = S // tq
    qtile = lambda w: pl.BlockSpec((None, tq, w), lambda b, j: (b, j, 0))
    seq = lambda w: pl.BlockSpec((None, S, w), lambda b, j: (b, 0, 0))
    return pl.pallas_call(
        functools.partial(_attn_kernel, tq=tq, topk=topk),
        grid=(B, nq),
        in_specs=[
            qtile(ATT_WIDTH), seq(ATT_WIDTH),
            pl.BlockSpec((None, nq, VT_ROWS, tq), lambda b, j: (b, 0, 0, 0)),
            qtile(IDX_HEADS * IDX_DIM), seq(2 * IDX_DIM),
            pl.BlockSpec((None, IW_ROWS, tq), lambda b, j: (b, 0, j)),
            qtile(ATT_WIDTH),
        ],
        out_specs=qtile(ATT_WIDTH),
        out_shape=jax.ShapeDtypeStruct((B, S, ATT_WIDTH), BF16),
        scratch_shapes=[
            pltpu.VMEM((nq, tq, tq), I32),
            pltpu.VMEM((nq, tq, tq), I16),
            pltpu.VMEM((nq, tq, tq), I16),
            pltpu.VMEM((nq, tq, tq), F32),
            pltpu.VMEM((ATT_HEADS, nq, tq, tq), F32),
            pltpu.VMEM((ATT_HEADS, 8, tq), F32),
            pltpu.VMEM((ATT_HEADS, PV_ROWS, tq), F32),
            pltpu.VMEM((ATT_WIDTH, tq), F32),
        ],
        compiler_params=pltpu.CompilerParams(
            dimension_semantics=("parallel", "arbitrary"), vmem_limit_bytes=VMEM_LIMIT),
        name="attn",
    )(q, k, vt, iq, ik, iwt, gaz)


def _mix_kernel(pu_ref, gpz_ref, gq_ref, gk_ref, gv_ref, ggz_ref, gg_ref,
                pw_ref, ps_ref, wg_ref, bg_ref, og_ref,
                yb_ref, yc_ref, pad_ref, st_ref, *, seq, rb):
    wmax = max(POOL_WINDOWS)
    pad_ref[0:wmax, :] = jnp.zeros((wmax, POOL_WIDTH), F32)
    pad_ref[wmax:wmax + seq, :] = pu_ref[...]
    lane = lax.broadcasted_iota(I32, (rb, POOL_WIDTH), 1)
    grp = lane // POOL_GROUP_DIM
    win = jnp.zeros((rb, POOL_WIDTH), I32)
    for g, w in enumerate(POOL_WINDOWS):
        win = jnp.where(grp == g, w, win)
    for blk in range(seq // rb):
        r0 = blk * rb
        run = jnp.zeros((rb, POOL_WIDTH), F32)
        wsum = jnp.zeros((rb, POOL_WIDTH), F32)
        for jj in range(wmax):
            run = run + pad_ref[wmax + r0 - jj:wmax + r0 - jj + rb, :]
            if (jj + 1) in POOL_WINDOWS:
                wsum = jnp.where(win == jj + 1, run, wsum)
        t1 = r0 + 1 + lax.broadcasted_iota(I32, (rb, POOL_WIDTH), 0)
        cnt = jnp.minimum(t1, win).astype(F32)
        pooled = wsum / cnt - pad_ref[wmax + r0:wmax + r0 + rb, :]
        y = jnp.dot(pooled.astype(BF16), pw_ref[...], preferred_element_type=F32)
        y = y * ps_ref[...] * gpz_ref[r0:r0 + rb, :]
        yb_ref[r0:r0 + rb, :] = y.astype(BF16)

    C = GLA_CHUNK
    H = GLA_HEADS
    ri = lax.broadcasted_iota(I32, (C, C), 0)
    ci = lax.broadcasted_iota(I32, (C, C), 1)
    tri = jnp.where(ri >= ci, 1.0, 0.0).astype(F32)
    klane = lax.broadcasted_iota(I32, (C, GLA_KEY_PAD), 1) // GLA_DK_PAD
    slane = lax.broadcasted_iota(I32, (GLA_DV_PAD, GLA_KEY_PAD), 1) // GLA_DK_PAD
    st_ref[...] = jnp.zeros((GLA_DV_PAD, GLA_KEY_PAD), F32)

    def chunk_body(n, carry):
        r0 = pl.multiple_of(n * C, C)
        z = jnp.dot(gg_ref[pl.ds(r0, C), :], wg_ref[...], preferred_element_type=F32,
                    precision=lax.Precision.HIGHEST) + bg_ref[...]
        g = jax.nn.log_sigmoid(z) * (1.0 / GLA_TAU)
        b = jnp.dot(tri, g, preferred_element_type=F32, precision=lax.Precision.HIGHEST)
        bend = b[C - 1:C, :]
        qc = gq_ref[pl.ds(r0, C), :]
        kcv = gk_ref[pl.ds(r0, C), :]
        v = gv_ref[pl.ds(r0, C), :]
        qt = qc * jnp.exp(b) * (GLA_DK ** -0.5)
        kt = (kcv * jnp.exp(-b)).astype(BF16)
        kd = (kcv * jnp.exp(bend - b)).astype(BF16)
        dec = jnp.exp(bend)
        qs = jnp.concatenate(
            [jnp.where(klane == h, qt, 0.0) for h in range(H)], axis=0).astype(BF16)
        st = st_ref[...]
        a = lax.dot_general(qs, kt, _NT, preferred_element_type=F32)
        o_inter = lax.dot_general(qs, st.astype(BF16), _NT, preferred_element_type=F32)
        for h in range(H):
            ah = jnp.where(ri >= ci, a[h * C:(h + 1) * C, :], 0.0).astype(BF16)
            vh = v[:, h * GLA_DV_PAD:(h + 1) * GLA_DV_PAD]
            o = jnp.dot(ah, vh, preferred_element_type=F32) + o_inter[h * C:(h + 1) * C, :]
            ms = jnp.sum(o * o, axis=-1, keepdims=True) * (1.0 / GLA_DV)
            y = o * lax.rsqrt(ms + EPS) * og_ref[...]
            y = y * ggz_ref[pl.ds(r0, C), h * GLA_DV_PAD:(h + 1) * GLA_DV_PAD]
            yc_ref[pl.ds(r0, C), h * GLA_DV_PAD:(h + 1) * GLA_DV_PAD] = y.astype(BF16)
        upd = lax.dot_general(v, kd, _TN, preferred_element_type=F32)
        new = dec * st
        for h in range(H):
            new = new + jnp.where(slane == h, upd[h * GLA_DV_PAD:(h + 1) * GLA_DV_PAD, :], 0.0)
        st_ref[...] = new
        return carry

    lax.fori_loop(0, seq // C, chunk_body, 0)


def _mix_call(pu, gpz, gq, gk, gv, ggz, gg, pw, ps, wg, bg, og):
    B, S, _ = pu.shape
    rb = min(256, S)
    seqb = lambda w: pl.BlockSpec((None, S, w), lambda b: (b, 0, 0))
    full = lambda a: pl.BlockSpec(a.shape, lambda b: (0,) * a.ndim)
    return pl.pallas_call(
        functools.partial(_mix_kernel, seq=S, rb=rb),
        grid=(B,),
        in_specs=[seqb(POOL_WIDTH), seqb(POOL_WIDTH), seqb(GLA_KEY_PAD), seqb(GLA_KEY_PAD),
                  seqb(GLA_VAL_PAD), seqb(GLA_VAL_PAD), seqb(GATE_PAD),
                  full(pw), full(ps), full(wg), full(bg), full(og)],
        out_specs=(seqb(POOL_WIDTH), seqb(GLA_VAL_PAD)),
        out_shape=(jax.ShapeDtypeStruct((B, S, POOL_WIDTH), BF16),
                   jax.ShapeDtypeStruct((B, S, GLA_VAL_PAD), BF16)),
        scratch_shapes=[
            pltpu.VMEM((max(POOL_WINDOWS) + S, POOL_WIDTH), F32),
            pltpu.VMEM((GLA_DV_PAD, GLA_KEY_PAD), F32),
        ],
        compiler_params=pltpu.CompilerParams(
            dimension_semantics=("parallel",), vmem_limit_bytes=VMEM_LIMIT),
        name="mix",
    )(pu, gpz, gq, gk, gv, ggz, gg, pw, ps, wg, bg, og)


def _out_kernel(x_ref, ya_ref, yb_ref, yc_ref, wa_ref, wb_ref, wc_ref, o_ref):
    y = jnp.dot(ya_ref[...], wa_ref[...], preferred_element_type=F32)
    y = y + jnp.dot(yb_ref[...], wb_ref[...], preferred_element_type=F32)
    y = y + jnp.dot(yc_ref[...], wc_ref[...], preferred_element_type=F32)
    o_ref[...] = x_ref[...] + y


def _out_call(x2, ya, yb, yc, wa, wb, wc, *, tm):
    T = x2.shape[0]
    row = lambda w: pl.BlockSpec((tm, w), lambda i: (i, 0))
    full = lambda a: pl.BlockSpec(a.shape, lambda i: (0,) * a.ndim)
    return pl.pallas_call(
        _out_kernel,
        grid=(T // tm,),
        in_specs=[row(D_MODEL), row(ATT_WIDTH), row(POOL_WIDTH), row(GLA_VAL_PAD),
                  full(wa), full(wb), full(wc)],
        out_specs=row(D_MODEL),
        out_shape=jax.ShapeDtypeStruct((T, D_MODEL), F32),
        compiler_params=pltpu.CompilerParams(
            dimension_semantics=("parallel",), vmem_limit_bytes=VMEM_LIMIT),
        name="outproj",
    )(x2, ya, yb, yc, wa, wb, wc)


def _pad_heads(w, heads, d, dpad, axis):
    shp = list(w.shape)
    w = w.reshape(shp[:axis] + [heads, d] + shp[axis + 1:])
    pad = [(0, 0)] * w.ndim
    pad[axis + 1] = (0, dpad - d)
    w = jnp.pad(w, pad)
    return w.reshape(shp[:axis] + [heads * dpad] + shp[axis + 1:])


def _pack_layer(norm_g, w_in, att_q_gain, att_k_gain, pool_w, pool_scale,
                gla_w_gate, gla_b_gate, gla_out_gain, w_out):
    offs = [0]
    for s in IN_SPLITS:
        offs.append(offs[-1] + s)
    (aq, ak, av, az, iq, ik, iw, pu, pz, gq, gk, gv, gg, gz) = [
        w_in[:, offs[i]:offs[i + 1]] for i in range(len(IN_SPLITS))]
    nat = {
        "q": aq, "k": ak, "z": az, "iq": iq,
        "ik2": jnp.concatenate([ik, ik], axis=1),
        "pu": pu, "pz": pz,
        "gq": _pad_heads(gq, GLA_HEADS, GLA_DK, GLA_DK_PAD, 1),
        "gk": _pad_heads(gk, GLA_HEADS, GLA_DK, GLA_DK_PAD, 1),
        "gv": _pad_heads(gv, GLA_HEADS, GLA_DV, GLA_DV_PAD, 1),
        "gz": _pad_heads(gz, GLA_HEADS, GLA_DV, GLA_DV_PAD, 1),
        "gg": jnp.pad(gg, ((0, 0), (0, GATE_PAD - GLA_GATE_RANK))),
    }
    wn = jnp.concatenate([nat[name] for name, _ in _NAT_PIECES], axis=1).astype(BF16)
    iw_t = jnp.pad(iw.T * (IDX_HEADS ** -0.5), ((0, IW_ROWS - IDX_HEADS), (0, 0)))
    wt = jnp.concatenate([av.T, iw_t], axis=0).astype(BF16)
    qg = (jnp.tile(att_q_gain, ATT_HEADS) * (ATT_HEAD_DIM ** -0.5)).reshape(1, ATT_WIDTH)
    kg = jnp.tile(att_k_gain, ATT_HEADS).reshape(1, ATT_WIDTH)
    hid = jnp.arange(ATT_WIDTH) // ATT_HEAD_DIM
    gm = jnp.where(hid[:, None] == hid[None, :], 1.0 / ATT_HEAD_DIM, 0.0).astype(BF16)
    pw = jax.scipy.linalg.block_diag(*[pool_w[g] for g in range(POOL_GROUPS)]).astype(BF16)
    ps = pool_scale.reshape(1, POOL_WIDTH)
    wg = jnp.pad(_pad_heads(gla_w_gate, GLA_HEADS, GLA_DK, GLA_DK_PAD, 1),
                 ((0, GATE_PAD - GLA_GATE_RANK), (0, 0)))
    bg = _pad_heads(gla_b_gate.reshape(1, -1), GLA_HEADS, GLA_DK, GLA_DK_PAD, 1)
    og = jnp.pad(gla_out_gain, (0, GLA_DV_PAD - GLA_DV)).reshape(1, GLA_DV_PAD)
    wa = w_out[0:ATT_WIDTH].astype(BF16)
    wb = w_out[ATT_WIDTH:ATT_WIDTH + POOL_WIDTH].astype(BF16)
    wc = _pad_heads(w_out[ATT_WIDTH + POOL_WIDTH:], GLA_HEADS, GLA_DV, GLA_DV_PAD, 0).astype(BF16)
    return dict(ng=norm_g.reshape(1, D_MODEL), wn=wn, wt=wt, qg=qg, kg=kg, gm=gm,
                pw=pw, ps=ps, wg=wg, bg=bg, og=og, wa=wa, wb=wb, wc=wc)


def _layer(x2, p, *, B, S, tm, tq, topk):
    (q, k, gaz, iq, ik, pu, gpz, gq, gk, gv, ggz, gg, vt, iwt) = _proj_call(
        x2, p["ng"], p["wn"], p["wt"], p["qg"], p["kg"], p["gm"], tm=tm, kc=tq)
    r3 = lambda a: a.reshape(B, S, a.shape[-1])
    ya = _attn_call(r3(q), r3(k), vt.reshape(B, S // tq, VT_ROWS, tq), r3(iq), r3(ik),
                    iwt.reshape(IW_ROWS, B, S).transpose(1, 0, 2), r3(gaz), tq=tq, topk=topk)
    yb, yc = _mix_call(r3(pu), r3(gpz), r3(gq), r3(gk), r3(gv), r3(ggz), r3(gg),
                       p["pw"], p["ps"], p["wg"], p["bg"], p["og"])
    T = B * S
    return _out_call(x2, ya.reshape(T, ATT_WIDTH), yb.reshape(T, POOL_WIDTH),
                     yc.reshape(T, GLA_VAL_PAD), p["wa"], p["wb"], p["wc"], tm=tm)


def _tiles(S):
    tq = min(256, S // 2)
    tm = min(512, S)
    return tm, tq


def kernel(x, norm_g, w_in, att_q_gain, att_k_gain, pool_w, pool_scale, gla_w_gate,
           gla_b_gate, gla_out_gain, w_out):
    B, S, _ = x.shape
    depth = norm_g.shape[0]
    tm, tq = _tiles(S)
    topk = min(TOPK_MAX, S // 4)
    x2 = x.reshape(B * S, D_MODEL)
    for l in range(depth):
        p = _pack_layer(norm_g[l], w_in[l], att_q_gain[l], att_k_gain[l], pool_w[l],
                        pool_scale[l], gla_w_gate[l], gla_b_gate[l], gla_out_gain[l], w_out[l])
        x2 = _layer(x2, p, B=B, S=S, tm=tm, tq=tq, topk=topk)
    return x2.reshape(B, S, D_MODEL)
```

```python
import functools

import jax
import jax.numpy as jnp
from jax import lax
from jax.experimental import pallas as pl
from jax.experimental.pallas import tpu as pltpu

F32 = jnp.float32
BF16 = jnp.bfloat16
I32 = jnp.int32
I16 = jnp.int16

D_MODEL = 1024
ATT_HEADS = 6
ATT_HEAD_DIM = 64
ATT_WIDTH = ATT_HEADS * ATT_HEAD_DIM
IDX_HEADS = 4
IDX_DIM = 64
TOPK_MAX = 256
POOL_GROUPS = 4
POOL_GROUP_DIM = 64
POOL_WIDTH = POOL_GROUPS * POOL_GROUP_DIM
POOL_WINDOWS = (2, 4, 8, 16)
GLA_HEADS = 4
GLA_DK = 48
GLA_DV = 96
GLA_KEY_WIDTH = GLA_HEADS * GLA_DK
GLA_VAL_WIDTH = GLA_HEADS * GLA_DV
GLA_GATE_RANK = 16
GLA_TAU = 16.0
GLA_CHUNK = 64
GLA_BLOCK = 256
EPS = 1e-6

IN_SPLITS = (
    ATT_WIDTH, ATT_WIDTH, ATT_WIDTH, ATT_WIDTH,
    IDX_HEADS * IDX_DIM, IDX_DIM, IDX_HEADS,
    POOL_WIDTH, POOL_WIDTH,
    GLA_KEY_WIDTH, GLA_KEY_WIDTH, GLA_VAL_WIDTH,
    GLA_GATE_RANK, GLA_VAL_WIDTH,
)

LANES = 128
BF16_ROWS = 16
GLA_DK_PAD = 64
GLA_DV_PAD = 128
GLA_KEY_PAD = GLA_HEADS * GLA_DK_PAD
GLA_VAL_PAD = GLA_HEADS * GLA_DV_PAD
GATE_PAD = LANES

_NAT_PIECES = (
    ("q", ATT_WIDTH), ("k", ATT_WIDTH), ("z", ATT_WIDTH),
    ("iq", IDX_HEADS * IDX_DIM), ("ik2", 2 * IDX_DIM),
    ("pu", POOL_WIDTH), ("pz", POOL_WIDTH),
    ("gq", GLA_KEY_PAD), ("gk", GLA_KEY_PAD), ("gv", GLA_VAL_PAD),
    ("gz", GLA_VAL_PAD), ("gg", GATE_PAD),
)
_NAT_OFF = {}
_off = 0
for _name, _w in _NAT_PIECES:
    _NAT_OFF[_name] = (_off, _off + _w)
    _off += _w
NAT_WIDTH = _off
VT_ROWS = ATT_WIDTH
IW_ROWS = 8
TRN_ROWS = VT_ROWS + IW_ROWS
PV_ROWS = ATT_HEAD_DIM + BF16_ROWS

INT_MIN = -(2 ** 31)
I16_MIN = -(2 ** 15)
NEG_BIG = -1e30
VMEM_LIMIT = 56 * 1024 * 1024

_NT = (((1,), (1,)), ((), ()))
_TN = (((0,), (0,)), ((), ()))


def _silu(x):
    return x * jax.nn.sigmoid(x)


def _fold_rows(x, rows, op):
    while x.shape[0] > rows:
        half = x.shape[0] // 2
        x = op(x[:half], x[half:])
    return x


def _proj_kernel(x_ref, ng_ref, wn_ref, wt_ref, qg_ref, kg_ref, gm_ref,
                 q_ref, k_ref, gaz_ref, iq_ref, ik_ref, pu_ref, gpz_ref,
                 gq_ref, gk_ref, gv_ref, ggz_ref, gg_ref, vt_ref, iwt_ref, *, kc):
    x = x_ref[...]
    ms = jnp.mean(x * x, axis=-1, keepdims=True)
    h = (x * lax.rsqrt(ms + EPS) * ng_ref[...]).astype(BF16)

    def nat(name):
        a, b = _NAT_OFF[name]
        return jnp.dot(h, wn_ref[:, a:b], preferred_element_type=F32)

    def head_norm(a, gain):
        sq = a * a
        hi = sq.astype(BF16)
        lo = (sq - hi.astype(F32)).astype(BF16)
        gm = gm_ref[...]
        hms = (jnp.dot(hi, gm, preferred_element_type=F32)
               + jnp.dot(lo, gm, preferred_element_type=F32))
        return a * lax.rsqrt(hms + EPS) * gain

    q_ref[...] = head_norm(nat("q"), qg_ref[...]).astype(BF16)
    k_ref[...] = head_norm(nat("k"), kg_ref[...]).astype(BF16)
    gaz_ref[...] = _silu(nat("z"))
    iq_ref[...] = nat("iq").astype(BF16)
    ik_ref[...] = nat("ik2").astype(BF16)
    pu_ref[...] = nat("pu")
    gpz_ref[...] = _silu(nat("pz"))
    gq_ref[...] = nat("gq")
    gk_ref[...] = nat("gk")
    gv_ref[...] = nat("gv").astype(BF16)
    ggz_ref[...] = _silu(nat("gz"))
    gg_ref[...] = nat("gg")

    pt = lax.dot_general(wt_ref[...], h, _NT, preferred_element_type=F32)
    tm = x.shape[0]
    for a in range(tm // kc):
        vt_ref[a] = pt[0:VT_ROWS, a * kc:(a + 1) * kc].astype(BF16)
    iwt_ref[...] = pt[VT_ROWS:TRN_ROWS, :]


def _proj_call(x2, ng, wn, wt, qg, kg, gm, *, tm, kc):
    T = x2.shape[0]
    grid = (T // tm,)
    row = lambda w: pl.BlockSpec((tm, w), lambda i: (i, 0))
    full = lambda a: pl.BlockSpec(a.shape, lambda i: (0,) * a.ndim)
    widths_dtypes = (
        (ATT_WIDTH, BF16), (ATT_WIDTH, BF16),
        (ATT_WIDTH, F32),
        (IDX_HEADS * IDX_DIM, BF16), (2 * IDX_DIM, BF16),
        (POOL_WIDTH, F32), (POOL_WIDTH, F32),
        (GLA_KEY_PAD, F32), (GLA_KEY_PAD, F32),
        (GLA_VAL_PAD, BF16), (GLA_VAL_PAD, F32),
        (GATE_PAD, F32),
    )
    out_shape = tuple(jax.ShapeDtypeStruct((T, w), dt) for w, dt in widths_dtypes) + (
        jax.ShapeDtypeStruct((T // kc, VT_ROWS, kc), BF16),
        jax.ShapeDtypeStruct((IW_ROWS, T), F32),
    )
    out_specs = tuple(row(w) for w, _ in widths_dtypes) + (
        pl.BlockSpec((tm // kc, VT_ROWS, kc), lambda i: (i, 0, 0)),
        pl.BlockSpec((IW_ROWS, tm), lambda i: (0, i)),
    )
    return pl.pallas_call(
        functools.partial(_proj_kernel, kc=kc),
        grid=grid,
        in_specs=[row(D_MODEL), full(ng), full(wn), full(wt), full(qg), full(kg), full(gm)],
        out_specs=out_specs,
        out_shape=out_shape,
        compiler_params=pltpu.CompilerParams(
            dimension_semantics=("parallel",), vmem_limit_bytes=VMEM_LIMIT),
        name="proj",
    )(x2, ng, wn, wt, qg, kg, gm)


def _attn_kernel(q_ref, k_ref, vt_ref, iq_ref, ik_ref, iwt_ref, gaz_ref, ya_ref,
                 k32_ref, khi_ref, klo_ref, bias_ref, lg_ref, mx_ref, racc_ref, ot_ref,
                 *, tq, topk):
    kc = tq
    j = pl.program_id(1)
    nch = j + 1
    lane = lax.broadcasted_iota(I32, (tq, LANES), 1)
    lo_half = lane < 64
    row_i = lax.broadcasted_iota(I32, (kc, tq), 0)
    col_i = lax.broadcasted_iota(I32, (kc, tq), 1)

    def half_mask(slab, h):
        keep = lo_half if h % 2 == 0 else jnp.logical_not(lo_half)
        return jnp.where(keep, slab, jnp.zeros_like(slab))

    iqm = [half_mask(iq_ref[:, LANES * (h // 2):LANES * (h // 2) + LANES], h)
           for h in range(IDX_HEADS)]
    iw = iwt_ref[...]

    def key_chunk(c, causal_mask):
        r0 = pl.multiple_of(c * kc, kc)
        ikc = ik_ref[pl.ds(r0, kc), :]
        acc = jnp.zeros((kc, tq), F32)
        for h in range(IDX_HEADS):
            d = lax.dot_general(ikc, iqm[h], _NT, preferred_element_type=F32)
            acc = acc + jnp.maximum(d, 0.0) * iw[h:h + 1, :]
        bits = lax.bitcast_convert_type(acc, I32)
        key = bits ^ ((bits >> 31) & 0x7FFFFFFF)
        if causal_mask:
            key = jnp.where(row_i <= col_i, key, INT_MIN)
        k32_ref[c] = key
        khi_ref[c] = (key >> 16).astype(I16)
        klo_ref[c] = ((key & 0xFFFF) + I16_MIN).astype(I16)

    def key_body(c, carry):
        key_chunk(c, False)
        return carry

    lax.fori_loop(0, j, key_body, 0)
    key_chunk(j, True)

    one16 = jnp.ones((kc, tq), I16)
    zero16 = jnp.zeros((kc, tq), I16)

    def count16(ref, pred):
        def body(c, cnt):
            ones = jnp.where(pred(ref[c]), one16, zero16)
            return cnt + _fold_rows(ones, BF16_ROWS, jnp.add)
        cnt = lax.fori_loop(0, nch, body, jnp.zeros((BF16_ROWS, tq), I16))
        return cnt.astype(I32).sum(axis=0, keepdims=True)

    def bisect16(ref, want):
        c0 = count16(ref, lambda u: u >= jnp.int16(0))
        cur0 = jnp.where(c0 >= want, 0, I16_MIN).astype(I32)

        def body(i, cur):
            cand = cur + jnp.left_shift(jnp.int32(1), 14 - i)
            cand16 = cand.astype(I16)
            cnt = count16(ref, lambda u: u >= cand16)
            return jnp.where(cnt >= want, cand, cur)

        return lax.fori_loop(0, 15, body, cur0)

    thr_hi = bisect16(khi_ref, topk)
    thr_hi16 = thr_hi.astype(I16)
    n_hi_gt = count16(khi_ref, lambda u: u > thr_hi16)

    def group_body(c, carry):
        klo_ref[c] = jnp.where(khi_ref[c] == thr_hi16, klo_ref[c], jnp.int16(I16_MIN))
        return carry

    lax.fori_loop(0, nch, group_body, 0)
    thr_lo = bisect16(klo_ref, topk - n_hi_gt)
    thr_lo16 = thr_lo.astype(I16)
    n_gt = n_hi_gt + count16(klo_ref, lambda u: u > thr_lo16)
    thr = (thr_hi << 16) | ((thr_lo - I16_MIN) & 0xFFFF)
    need = jnp.where(thr == INT_MIN, 0, topk - n_gt).astype(F32)
    n_eq = count16(klo_ref, lambda u: u == thr_lo16)
    excess = jnp.where(thr == INT_MIN, 0, n_eq - (topk - n_gt))
    has_excess_ties = jnp.max(excess) > 0

    def bias_with_ties():
        tri = jnp.where(row_i >= col_i, 1.0, 0.0).astype(BF16)

        def body(c, run):
            u = k32_ref[c]
            tie = u == thr
            tie_f = jnp.where(tie, 1.0, 0.0).astype(BF16)
            rank = run + jnp.dot(tri, tie_f, preferred_element_type=F32)
            take_tie = jnp.where(tie, jnp.where(rank <= need, 0.0, NEG_BIG), NEG_BIG)
            bias_ref[c] = jnp.where(u > thr, 0.0, take_tie)
            return rank[kc - 1:kc, :]

        lax.fori_loop(0, nch, body, jnp.zeros((1, tq), F32))

    def bias_no_ties():
        thr_eff = jnp.maximum(thr, INT_MIN + 1)

        def body(c, carry):
            bias_ref[c] = jnp.where(k32_ref[c] >= thr_eff, 0.0, NEG_BIG)
            return carry

        lax.fori_loop(0, nch, body, 0)

    lax.cond(has_excess_ties, bias_with_ties, bias_no_ties)

    qm = [half_mask(q_ref[:, LANES * (h // 2):LANES * (h // 2) + LANES], h)
          for h in range(ATT_HEADS)]
    mx_ref[...] = jnp.full(mx_ref.shape, NEG_BIG, F32)
    racc_ref[...] = jnp.zeros(racc_ref.shape, F32)

    def logit_body(c, carry):
        r0 = pl.multiple_of(c * kc, kc)
        bias = bias_ref[c]
        for h in range(ATT_HEADS):
            p = h // 2
            kch = k_ref[pl.ds(r0, kc), LANES * p:LANES * p + LANES]
            lg = lax.dot_general(kch, qm[h], _NT, preferred_element_type=F32) + bias
            lg_ref[h, c] = lg
            mx_ref[h] = jnp.maximum(mx_ref[h], _fold_rows(lg, 8, jnp.maximum))
        return carry

    lax.fori_loop(0, nch, logit_body, 0)
    mrow = [jnp.max(mx_ref[h], axis=0, keepdims=True) for h in range(ATT_HEADS)]
    ones_rows = jnp.ones((BF16_ROWS, kc), BF16)

    def pv_body(c, carry):
        for h in range(ATT_HEADS):
            vaug = jnp.concatenate(
                [vt_ref[c, ATT_HEAD_DIM * h:ATT_HEAD_DIM * (h + 1), :], ones_rows], axis=0)
            pr = jnp.exp(lg_ref[h, c] - mrow[h]).astype(BF16)
            racc_ref[h] += jnp.dot(vaug, pr, preferred_element_type=F32)
        return carry

    lax.fori_loop(0, nch, pv_body, 0)

    for h in range(ATT_HEADS):
        r = racc_ref[h]
        ot_ref[ATT_HEAD_DIM * h:ATT_HEAD_DIM * (h + 1), :] = (
            r[0:ATT_HEAD_DIM, :] / r[ATT_HEAD_DIM:ATT_HEAD_DIM + 1, :])
    ya_ref[...] = (ot_ref[...].T * gaz_ref[...]).astype(BF16)


def _attn_call(q, k, vt, iq, ik, iwt, gaz, *, tq, topk):
    B, S, _ = q.shape
    nq = S // tq
    qtile = lambda w: pl.BlockSpec((None, tq, w), lambda b, j: (b, j, 0))
    seq = lambda w: pl.BlockSpec((None, S, w), lambda b, j: (b, 0, 0))
    return pl.pallas_call(
        functools.partial(_attn_kernel, tq=tq, topk=topk),
        grid=(B, nq),
        in_specs=[
            qtile(ATT_WIDTH), seq(ATT_WIDTH),
            pl.BlockSpec((None, nq, VT_ROWS, tq), lambda b, j: (b, 0, 0, 0)),
            qtile(IDX_HEADS * IDX_DIM), seq(2 * IDX_DIM),
            pl.BlockSpec((None, IW_ROWS, tq), lambda b, j: (b, 0, j)),
            qtile(ATT_WIDTH),
        ],
        out_specs=qtile(ATT_WIDTH),
        out_shape=jax.ShapeDtypeStruct((B, S, ATT_WIDTH), BF16),
        scratch_shapes=[
            pltpu.VMEM((nq, tq, tq), I32),
            pltpu.VMEM((nq, tq, tq), I16),
            pltpu.VMEM((nq, tq, tq), I16),
            pltpu.VMEM((nq, tq, tq), F32),
            pltpu.VMEM((ATT_HEADS, nq, tq, tq), F32),
            pltpu.VMEM((ATT_HEADS, 8, tq), F32),
            pltpu.VMEM((ATT_HEADS, PV_ROWS, tq), F32),
            pltpu.VMEM((ATT_WIDTH, tq), F32),
        ],
        compiler_params=pltpu.CompilerParams(
            dimension_semantics=("parallel", "arbitrary"), vmem_limit_bytes=VMEM_LIMIT),
        name="attn",
    )(q, k, vt, iq, ik, iwt, gaz)


def _mix_kernel(pu_ref, gpz_ref, gq_ref, gk_ref, gv_ref, ggz_ref, gg_ref,
                pw_ref, ps_ref, wg_ref, bg_ref, og_ref,
                yb_ref, yc_ref, pad_ref, qt_ref, kt_ref, kd_ref, dec_ref, sts_ref, *, seq, rb):
    wmax = max(POOL_WINDOWS)
    pad_ref[0:wmax, :] = jnp.zeros((wmax, POOL_WIDTH), F32)
    pad_ref[wmax:wmax + seq, :] = pu_ref[...]
    lane = lax.broadcasted_iota(I32, (rb, POOL_WIDTH), 1)
    grp = lane // POOL_GROUP_DIM
    win = jnp.zeros((rb, POOL_WIDTH), I32)
    for g, w in enumerate(POOL_WINDOWS):
        win = jnp.where(grp == g, w, win)
    for blk in range(seq // rb):
        r0 = blk * rb
        run = jnp.zeros((rb, POOL_WIDTH), F32)
        wsum = jnp.zeros((rb, POOL_WIDTH), F32)
        for jj in range(wmax):
            run = run + pad_ref[wmax + r0 - jj:wmax + r0 - jj + rb, :]
            if (jj + 1) in POOL_WINDOWS:
                wsum = jnp.where(win == jj + 1, run, wsum)
        t1 = r0 + 1 + lax.broadcasted_iota(I32, (rb, POOL_WIDTH), 0)
        cnt = jnp.minimum(t1, win).astype(F32)
        pooled = wsum / cnt - pad_ref[wmax + r0:wmax + r0 + rb, :]
        y = jnp.dot(pooled.astype(BF16), pw_ref[...], preferred_element_type=F32)
        y = y * ps_ref[...] * gpz_ref[r0:r0 + rb, :]
        yb_ref[r0:r0 + rb, :] = y.astype(BF16)

    C = GLA_CHUNK
    H = GLA_HEADS
    NB = GLA_BLOCK
    bi = lax.broadcasted_iota(I32, (NB, NB), 0)
    bj = lax.broadcasted_iota(I32, (NB, NB), 1)
    tri_blk = jnp.where((bi >= bj) & (bi // C == bj // C), 1.0, 0.0).astype(F32)

    def gate_body(blk, carry):
        r0 = pl.multiple_of(blk * NB, NB)
        z = jnp.dot(gg_ref[pl.ds(r0, NB), :], wg_ref[...], preferred_element_type=F32,
                    precision=lax.Precision.HIGHEST) + bg_ref[...]
        g = jax.nn.log_sigmoid(z) * (1.0 / GLA_TAU)
        b = jnp.dot(tri_blk, g, preferred_element_type=F32, precision=lax.Precision.HIGHEST)
        for cc in range(NB // C):
            rows = pl.ds(r0 + cc * C, C)
            bc = b[cc * C:(cc + 1) * C, :]
            bend = bc[C - 1:C, :]
            qc = gq_ref[rows, :]
            kcv = gk_ref[rows, :]
            qt_ref[rows, :] = (qc * jnp.exp(bc) * (GLA_DK ** -0.5)).astype(BF16)
            kt_ref[rows, :] = (kcv * jnp.exp(-bc)).astype(BF16)
            kd_ref[rows, :] = (kcv * jnp.exp(bend - bc)).astype(BF16)
            dec_ref[pl.ds(pl.multiple_of((r0 + cc * C) // 8, 8), 8), :] = jnp.broadcast_to(
                jnp.exp(bend), (8, GLA_KEY_PAD))
        return carry

    lax.fori_loop(0, seq // NB, gate_body, 0, unroll=2)

    slane = lax.broadcasted_iota(I32, (GLA_DV_PAD, GLA_KEY_PAD), 1) // GLA_DK_PAD

    def state_body(n, st):
        r0 = pl.multiple_of(n * C, C)
        v = gv_ref[pl.ds(r0, C), :]
        kd = kd_ref[pl.ds(r0, C), :]
        upd = lax.dot_general(v, kd, _TN, preferred_element_type=F32)
        sts_ref[n] = st.astype(BF16)
        new = dec_ref[pl.ds(pl.multiple_of(n * 8, 8), 1), :] * st
        for h in range(H):
            new = new + jnp.where(slane == h, upd[h * GLA_DV_PAD:(h + 1) * GLA_DV_PAD, :], 0.0)
        return new

    lax.fori_loop(0, seq // C, state_body, jnp.zeros((GLA_DV_PAD, GLA_KEY_PAD), F32), unroll=4)

    ri = lax.broadcasted_iota(I32, (C, C), 0)
    ci = lax.broadcasted_iota(I32, (C, C), 1)
    klane = lax.broadcasted_iota(I32, (C, GLA_KEY_PAD), 1) // GLA_DK_PAD

    def out_body(n, carry):
        r0 = pl.multiple_of(n * C, C)
        qt = qt_ref[pl.ds(r0, C), :]
        v = gv_ref[pl.ds(r0, C), :]
        qs = jnp.concatenate(
            [jnp.where(klane == h, qt, jnp.zeros_like(qt)) for h in range(H)], axis=0)
        a = lax.dot_general(qs, kt_ref[pl.ds(r0, C), :], _NT, preferred_element_type=F32)
        o_inter = lax.dot_general(qs, sts_ref[n], _NT, preferred_element_type=F32)
        for h in range(H):
            ah = jnp.where(ri >= ci, a[h * C:(h + 1) * C, :], 0.0).astype(BF16)
            vh = v[:, h * GLA_DV_PAD:(h + 1) * GLA_DV_PAD]
            o = jnp.dot(ah, vh, preferred_element_type=F32) + o_inter[h * C:(h + 1) * C, :]
            ms = jnp.sum(o * o, axis=-1, keepdims=True) * (1.0 / GLA_DV)
            y = o * lax.rsqrt(ms + EPS) * og_ref[...]
            y = y * ggz_ref[pl.ds(r0, C), h * GLA_DV_PAD:(h + 1) * GLA_DV_PAD]
            yc_ref[pl.ds(r0, C), h * GLA_DV_PAD:(h + 1) * GLA_DV_PAD] = y.astype(BF16)
        return carry

    lax.fori_loop(0, seq // C, out_body, 0, unroll=4)


def _mix_call(pu, gpz, gq, gk, gv, ggz, gg, pw, ps, wg, bg, og):
    B, S, _ = pu.shape
    rb = min(256, S)
    seqb = lambda w: pl.BlockSpec((None, S, w), lambda b: (b, 0, 0))
    full = lambda a: pl.BlockSpec(a.shape, lambda b: (0,) * a.ndim)
    return pl.pallas_call(
        functools.partial(_mix_kernel, seq=S, rb=rb),
        grid=(B,),
        in_specs=[seqb(POOL_WIDTH), seqb(POOL_WIDTH), seqb(GLA_KEY_PAD), seqb(GLA_KEY_PAD),
                  seqb(GLA_VAL_PAD), seqb(GLA_VAL_PAD), seqb(GATE_PAD),
                  full(pw), full(ps), full(wg), full(bg), full(og)],
        out_specs=(seqb(POOL_WIDTH), seqb(GLA_VAL_PAD)),
        out_shape=(jax.ShapeDtypeStruct((B, S, POOL_WIDTH), BF16),
                   jax.ShapeDtypeStruct((B, S, GLA_VAL_PAD), BF16)),
        scratch_shapes=[
            pltpu.VMEM((max(POOL_WINDOWS) + S, POOL_WIDTH), F32),
            pltpu.VMEM((S, GLA_KEY_PAD), BF16),
            pltpu.VMEM((S, GLA_KEY_PAD), BF16),
            pltpu.VMEM((S, GLA_KEY_PAD), BF16),
            pltpu.VMEM((S // GLA_CHUNK * 8, GLA_KEY_PAD), F32),
            pltpu.VMEM((S // GLA_CHUNK, GLA_DV_PAD, GLA_KEY_PAD), BF16),
        ],
        compiler_params=pltpu.CompilerParams(
            dimension_semantics=("parallel",), vmem_limit_bytes=VMEM_LIMIT),
        name="mix",
    )(pu, gpz, gq, gk, gv, ggz, gg, pw, ps, wg, bg, og)


def _out_kernel(x_ref, ya_ref, yb_ref, yc_ref, wa_ref, wb_ref, wc_ref, o_ref):
    y = jnp.dot(ya_ref[...], wa_ref[...], preferred_element_type=F32)
    y = y + jnp.dot(yb_ref[...], wb_ref[...], preferred_element_type=F32)
    y = y + jnp.dot(yc_ref[...], wc_ref[...], preferred_element_type=F32)
    o_ref[...] = x_ref[...] + y


def _out_call(x2, ya, yb, yc, wa, wb, wc, *, tm):
    T = x2.shape[0]
    row = lambda w: pl.BlockSpec((tm, w), lambda i: (i, 0))
    full = lambda a: pl.BlockSpec(a.shape, lambda i: (0,) * a.ndim)
    return pl.pallas_call(
        _out_kernel,
        grid=(T // tm,),
        in_specs=[row(D_MODEL), row(ATT_WIDTH), row(POOL_WIDTH), row(GLA_VAL_PAD),
                  full(wa), full(wb), full(wc)],
        out_specs=row(D_MODEL),
        out_shape=jax.ShapeDtypeStruct((T, D_MODEL), F32),
        compiler_params=pltpu.CompilerParams(
            dimension_semantics=("parallel",), vmem_limit_bytes=VMEM_LIMIT),
        name="outproj",
    )(x2, ya, yb, yc, wa, wb, wc)


def _pad_heads(w, heads, d, dpad, axis):
    shp = list(w.shape)
    w = w.reshape(shp[:axis] + [heads, d] + shp[axis + 1:])
    pad = [(0, 0)] * w.ndim
    pad[axis + 1] = (0, dpad - d)
    w = jnp.pad(w, pad)
    return w.reshape(shp[:axis] + [heads * dpad] + shp[axis + 1:])


def _pack_layer(norm_g, w_in, att_q_gain, att_k_gain, pool_w, pool_scale,
                gla_w_gate, gla_b_gate, gla_out_gain, w_out):
    offs = [0]
    for s in IN_SPLITS:
        offs.append(offs[-1] + s)
    (aq, ak, av, az, iq, ik, iw, pu, pz, gq, gk, gv, gg, gz) = [
        w_in[:, offs[i]:offs[i + 1]] for i in range(len(IN_SPLITS))]
    nat = {
        "q": aq, "k": ak, "z": az, "iq": iq,
        "ik2": jnp.concatenate([ik, ik], axis=1),
        "pu": pu, "pz": pz,
        "gq": _pad_heads(gq, GLA_HEADS, GLA_DK, GLA_DK_PAD, 1),
        "gk": _pad_heads(gk, GLA_HEADS, GLA_DK, GLA_DK_PAD, 1),
        "gv": _pad_heads(gv, GLA_HEADS, GLA_DV, GLA_DV_PAD, 1),
        "gz": _pad_heads(gz, GLA_HEADS, GLA_DV, GLA_DV_PAD, 1),
        "gg": jnp.pad(gg, ((0, 0), (0, GATE_PAD - GLA_GATE_RANK))),
    }
    wn = jnp.concatenate([nat[name] for name, _ in _NAT_PIECES], axis=1).astype(BF16)
    iw_t = jnp.pad(iw.T * (IDX_HEADS ** -0.5), ((0, IW_ROWS - IDX_HEADS), (0, 0)))
    wt = jnp.concatenate([av.T, iw_t], axis=0).astype(BF16)
    qg = (jnp.tile(att_q_gain, ATT_HEADS) * (ATT_HEAD_DIM ** -0.5)).reshape(1, ATT_WIDTH)
    kg = jnp.tile(att_k_gain, ATT_HEADS).reshape(1, ATT_WIDTH)
    hid = jnp.arange(ATT_WIDTH) // ATT_HEAD_DIM
    gm = jnp.where(hid[:, None] == hid[None, :], 1.0 / ATT_HEAD_DIM, 0.0).astype(BF16)
    pw = jax.scipy.linalg.block_diag(*[pool_w[g] for g in range(POOL_GROUPS)]).astype(BF16)
    ps = pool_scale.reshape(1, POOL_WIDTH)
    wg = jnp.pad(_pad_heads(gla_w_gate, GLA_HEADS, GLA_DK, GLA_DK_PAD, 1),
                 ((0, GATE_PAD - GLA_GATE_RANK), (0, 0)))
    bg = _pad_heads(gla_b_gate.reshape(1, -1), GLA_HEADS, GLA_DK, GLA_DK_PAD, 1)
    og = jnp.pad(gla_out_gain, (0, GLA_DV_PAD - GLA_DV)).reshape(1, GLA_DV_PAD)
    wa = w_out[0:ATT_WIDTH].astype(BF16)
    wb = w_out[ATT_WIDTH:ATT_WIDTH + POOL_WIDTH].astype(BF16)
    wc = _pad_heads(w_out[ATT_WIDTH + POOL_WIDTH:], GLA_HEADS, GLA_DV, GLA_DV_PAD, 0).astype(BF16)
    return dict(ng=norm_g.reshape(1, D_MODEL), wn=wn, wt=wt, qg=qg, kg=kg, gm=gm,
                pw=pw, ps=ps, wg=wg, bg=bg, og=og, wa=wa, wb=wb, wc=wc)


def _layer(x2, p, *, B, S, tm, tq, topk):
    (q, k, gaz, iq, ik, pu, gpz, gq, gk, gv, ggz, gg, vt, iwt) = _proj_call(
        x2, p["ng"], p["wn"], p["wt"], p["qg"], p["kg"], p["gm"], tm=tm, kc=tq)
    r3 = lambda a: a.reshape(B, S, a.shape[-1])
    ya = _attn_call(r3(q), r3(k), vt.reshape(B, S // tq, VT_ROWS, tq), r3(iq), r3(ik),
                    iwt.reshape(IW_ROWS, B, S).transpose(1, 0, 2), r3(gaz), tq=tq, topk=topk)
    yb, yc = _mix_call(r3(pu), r3(gpz), r3(gq), r3(gk), r3(gv), r3(ggz), r3(gg),
                       p["pw"], p["ps"], p["wg"], p["bg"], p["og"])
    T = B * S
    return _out_call(x2, ya.reshape(T, ATT_WIDTH), yb.reshape(T, POOL_WIDTH),
                     yc.reshape(T, GLA_VAL_PAD), p["wa"], p["wb"], p["wc"], tm=tm)


def _tiles(S):
    tq = min(256, S // 2)
    tm = min(512, S)
    return tm, tq


def kernel(x, norm_g, w_in, att_q_gain, att_k_gain, pool_w, pool_scale, gla_w_gate,
           gla_b_gate, gla_out_gain, w_out):
    B, S, _ = x.shape
    depth = norm_g.shape[0]
    tm, tq = _tiles(S)
    topk = min(TOPK_MAX, S // 4)
    x2 = x.reshape(B * S, D_MODEL)
    for l in range(depth):
        p = _pack_layer(norm_g[l], w_in[l], att_q_gain[l], att_k_gain[l], pool_w[l],
                        pool_scale[l], gla_w_gate[l], gla_b_gate[l], gla_out_gain[l], w_out[l])
        x2 = _layer(x2, p, B=B, S=S, tm=tm, tq=tq, topk=topk)
    return x2.reshape(B, S, D_MODEL)
```

```python
import functools

import jax
import jax.numpy as jnp
from jax import lax
from jax.experimental import pallas as pl
from jax.experimental.pallas import tpu as pltpu

F32 = jnp.float32
BF16 = jnp.bfloat16
I32 = jnp.int32

D_MODEL = 1024
ATT_HEADS = 6
ATT_HEAD_DIM = 64
ATT_WIDTH = ATT_HEADS * ATT_HEAD_DIM
IDX_HEADS = 4
IDX_DIM = 64
TOPK_MAX = 256
POOL_GROUPS = 4
POOL_GROUP_DIM = 64
POOL_WIDTH = POOL_GROUPS * POOL_GROUP_DIM
POOL_WINDOWS = (2, 4, 8, 16)
GLA_HEADS = 4
GLA_DK = 48
GLA_DV = 96
GLA_KEY_WIDTH = GLA_HEADS * GLA_DK
GLA_VAL_WIDTH = GLA_HEADS * GLA_DV
GLA_GATE_RANK = 16
GLA_TAU = 16.0
GLA_CHUNK = 64
GLA_BLOCK = 256
EPS = 1e-6

IN_SPLITS = (
    ATT_WIDTH, ATT_WIDTH, ATT_WIDTH, ATT_WIDTH,
    IDX_HEADS * IDX_DIM, IDX_DIM, IDX_HEADS,
    POOL_WIDTH, POOL_WIDTH,
    GLA_KEY_WIDTH, GLA_KEY_WIDTH, GLA_VAL_WIDTH,
    GLA_GATE_RANK, GLA_VAL_WIDTH,
)

LANES = 128
BF16_ROWS = 16
GLA_DK_PAD = 64
GLA_DV_PAD = 128
GLA_KEY_PAD = GLA_HEADS * GLA_DK_PAD
GLA_VAL_PAD = GLA_HEADS * GLA_DV_PAD
GATE_PAD = LANES

_NAT_PIECES = (
    ("q", ATT_WIDTH), ("k", ATT_WIDTH), ("z", ATT_WIDTH),
    ("iq", IDX_HEADS * IDX_DIM), ("ik2", 2 * IDX_DIM),
    ("pu", POOL_WIDTH), ("pz", POOL_WIDTH),
    ("gq", GLA_KEY_PAD), ("gk", GLA_KEY_PAD), ("gv", GLA_VAL_PAD),
    ("gz", GLA_VAL_PAD), ("gg", GATE_PAD),
)
_NAT_OFF = {}
_off = 0
for _name, _w in _NAT_PIECES:
    _NAT_OFF[_name] = (_off, _off + _w)
    _off += _w
NAT_WIDTH = _off
VT_ROWS = ATT_WIDTH
IW_ROWS = 8
TRN_ROWS = VT_ROWS + IW_ROWS
PV_ROWS = ATT_HEAD_DIM + BF16_ROWS

INT_MIN = -(2 ** 31)
KEY_BITS = 32
_TRANSPOSE_MASKS = {16: 0x0000FFFF, 8: 0x00FF00FF, 4: 0x0F0F0F0F, 2: 0x33333333, 1: 0x55555555}
NEG_BIG = -1e30
VMEM_LIMIT = 56 * 1024 * 1024

_NT = (((1,), (1,)), ((), ()))
_TN = (((0,), (0,)), ((), ()))


def _silu(x):
    return x * jax.nn.sigmoid(x)


def _fold_rows(x, rows, op):
    while x.shape[0] > rows:
        half = x.shape[0] // 2
        x = op(x[:half], x[half:])
    return x


def _proj_kernel(x_ref, ng_ref, wn_ref, wt_ref, qg_ref, kg_ref, gm_ref,
                 q_ref, k_ref, gaz_ref, iq_ref, ik_ref, pu_ref, gpz_ref,
                 gq_ref, gk_ref, gv_ref, ggz_ref, gg_ref, vt_ref, iwt_ref, *, kc):
    x = x_ref[...]
    ms = jnp.mean(x * x, axis=-1, keepdims=True)
    h = (x * lax.rsqrt(ms + EPS) * ng_ref[...]).astype(BF16)

    def nat(name):
        a, b = _NAT_OFF[name]
        return jnp.dot(h, wn_ref[:, a:b], preferred_element_type=F32)

    def head_norm(a, gain):
        hms = jnp.dot((a * a).astype(BF16), gm_ref[...], preferred_element_type=F32)
        return a * lax.rsqrt(hms + EPS) * gain

    q_ref[...] = head_norm(nat("q"), qg_ref[...]).astype(BF16)
    k_ref[...] = head_norm(nat("k"), kg_ref[...]).astype(BF16)
    gaz_ref[...] = _silu(nat("z"))
    iq_ref[...] = nat("iq").astype(BF16)
    ik_ref[...] = nat("ik2").astype(BF16)
    pu_ref[...] = nat("pu")
    gpz_ref[...] = _silu(nat("pz"))
    gq_ref[...] = nat("gq")
    gk_ref[...] = nat("gk")
    gv_ref[...] = nat("gv").astype(BF16)
    ggz_ref[...] = _silu(nat("gz"))
    gg_ref[...] = nat("gg")

    pt = lax.dot_general(wt_ref[...], h, _NT, preferred_element_type=F32)
    tm = x.shape[0]
    for a in range(tm // kc):
        vt_ref[a] = pt[0:VT_ROWS, a * kc:(a + 1) * kc].astype(BF16)
    iwt_ref[...] = pt[VT_ROWS:TRN_ROWS, :]


def _proj_call(x2, ng, wn, wt, qg, kg, gm, *, tm, kc):
    T = x2.shape[0]
    grid = (T // tm,)
    row = lambda w: pl.BlockSpec((tm, w), lambda i: (i, 0))
    full = lambda a: pl.BlockSpec(a.shape, lambda i: (0,) * a.ndim)
    widths_dtypes = (
        (ATT_WIDTH, BF16), (ATT_WIDTH, BF16),
        (ATT_WIDTH, F32),
        (IDX_HEADS * IDX_DIM, BF16), (2 * IDX_DIM, BF16),
        (POOL_WIDTH, F32), (POOL_WIDTH, F32),
        (GLA_KEY_PAD, F32), (GLA_KEY_PAD, F32),
        (GLA_VAL_PAD, BF16), (GLA_VAL_PAD, F32),
        (GATE_PAD, F32),
    )
    out_shape = tuple(jax.ShapeDtypeStruct((T, w), dt) for w, dt in widths_dtypes) + (
        jax.ShapeDtypeStruct((T // kc, VT_ROWS, kc), BF16),
        jax.ShapeDtypeStruct((IW_ROWS, T), F32),
    )
    out_specs = tuple(row(w) for w, _ in widths_dtypes) + (
        pl.BlockSpec((tm // kc, VT_ROWS, kc), lambda i: (i, 0, 0)),
        pl.BlockSpec((IW_ROWS, tm), lambda i: (0, i)),
    )
    return pl.pallas_call(
        functools.partial(_proj_kernel, kc=kc),
        grid=grid,
        in_specs=[row(D_MODEL), full(ng), full(wn), full(wt), full(qg), full(kg), full(gm)],
        out_specs=out_specs,
        out_shape=out_shape,
        compiler_params=pltpu.CompilerParams(
            dimension_semantics=("parallel",), vmem_limit_bytes=VMEM_LIMIT),
        name="proj",
    )(x2, ng, wn, wt, qg, kg, gm)


def _attn_kernel(q_ref, k_ref, vt_ref, iq_ref, ik_ref, iwt_ref, gaz_ref, ya_ref,
                 k32_ref, planes_ref, alive_ref, bias_ref, lg_ref, mx_ref, racc_ref, ot_ref,
                 *, tq, kc, nk, topk):
    j = pl.program_id(1)
    ndiag = tq // kc
    nfull = j * ndiag
    nch = nfull + ndiag
    lane = lax.broadcasted_iota(I32, (tq, LANES), 1)
    lo_half = lane < 64
    row_i = lax.broadcasted_iota(I32, (kc, tq), 0)
    col_i = lax.broadcasted_iota(I32, (kc, tq), 1)

    def half_mask(slab, h):
        keep = lo_half if h % 2 == 0 else jnp.logical_not(lo_half)
        return jnp.where(keep, slab, jnp.zeros_like(slab))

    @pl.when((pl.program_id(0) == 0) & (j == 0))
    def _():
        planes_ref[...] = jnp.zeros(planes_ref.shape, I32)

    iqm = [half_mask(iq_ref[:, LANES * (h // 2):LANES * (h // 2) + LANES], h)
           for h in range(IDX_HEADS)]
    iw = iwt_ref[...]

    def key_chunk(c, causal_mask):
        r0 = pl.multiple_of(c * kc, kc)
        ikc = ik_ref[pl.ds(r0, kc), :]
        acc = jnp.zeros((kc, tq), F32)
        for h in range(IDX_HEADS):
            d = lax.dot_general(ikc, iqm[h], _NT, preferred_element_type=F32)
            acc = acc + jnp.maximum(d, 0.0) * iw[h:h + 1, :]
        bits = lax.bitcast_convert_type(acc, I32)
        key = bits ^ ((bits >> 31) & 0x7FFFFFFF)
        if causal_mask:
            key = jnp.where(r0 + row_i <= j * tq + col_i, key, INT_MIN)
        k32_ref[c] = key
        u = key ^ INT_MIN
        xs = [u[8 * v:8 * v + 8, :] for v in range(KEY_BITS)]
        sh = KEY_BITS // 2
        while sh:
            msk = _TRANSPOSE_MASKS[sh]
            for a in range(KEY_BITS):
                if a & sh == 0:
                    t = (xs[a] ^ lax.shift_right_logical(xs[a + sh], sh)) & msk
                    xs[a] = xs[a] ^ t
                    xs[a + sh] = xs[a + sh] ^ (t << sh)
            sh //= 2
        for i in range(KEY_BITS):
            planes_ref[c, i] = xs[i]

    def key_body(c, carry):
        key_chunk(c, False)
        return carry

    lax.fori_loop(0, nfull, key_body, 0)
    for dd in range(ndiag):
        key_chunk(nfull + dd, True)

    for c in range(nk):
        alive_ref[c] = jnp.where(c < nch, jnp.full((8, tq), -1, I32), jnp.zeros((8, tq), I32))

    def radix_body(i, carry):
        need_i, thr_u = carry
        cnt = jnp.zeros((8, tq), I32)
        for c in range(nk):
            cnt = cnt + lax.population_count(alive_ref[c] & planes_ref[c, i])
        c1 = cnt.sum(axis=0, keepdims=True)
        take1 = c1 >= need_i
        for c in range(nk):
            al = alive_ref[c]
            ones = al & planes_ref[c, i]
            alive_ref[c] = jnp.where(take1, ones, al ^ ones)
        bit = lax.shift_right_logical(jnp.int32(INT_MIN), i)
        return jnp.where(take1, need_i, need_i - c1), jnp.where(take1, thr_u | bit, thr_u)

    need_i, thr_u = lax.fori_loop(
        0, KEY_BITS, radix_body, (jnp.full((1, tq), topk, I32), jnp.zeros((1, tq), I32)))
    thr = thr_u ^ INT_MIN
    n_eq = jnp.zeros((8, tq), I32)
    for c in range(nk):
        n_eq = n_eq + lax.population_count(alive_ref[c])
    n_eq = n_eq.sum(axis=0, keepdims=True)
    need = jnp.where(thr == INT_MIN, 0, need_i).astype(F32)
    excess = jnp.where(thr == INT_MIN, 0, n_eq - need_i)
    has_excess_ties = jnp.max(excess) > 0

    def bias_with_ties():
        tri = jnp.where(lax.broadcasted_iota(I32, (kc, kc), 0) >= lax.broadcasted_iota(I32, (kc, kc), 1),
                        1.0, 0.0).astype(BF16)

        def body(c, run):
            u = k32_ref[c]
            tie = u == thr
            tie_f = jnp.where(tie, 1.0, 0.0).astype(BF16)
            rank = run + jnp.dot(tri, tie_f, preferred_element_type=F32)
            take_tie = jnp.where(tie, jnp.where(rank <= need, 0.0, NEG_BIG), NEG_BIG)
            bias_ref[c] = jnp.where(u > thr, 0.0, take_tie)
            return rank[kc - 1:kc, :]

        lax.fori_loop(0, nch, body, jnp.zeros((1, tq), F32))

    def bias_no_ties():
        thr_eff = jnp.maximum(thr, INT_MIN + 1)

        def body(c, carry):
            bias_ref[c] = jnp.where(k32_ref[c] >= thr_eff, 0.0, NEG_BIG)
            return carry

        lax.fori_loop(0, nch, body, 0)

    lax.cond(has_excess_ties, bias_with_ties, bias_no_ties)

    qm = [half_mask(q_ref[:, LANES * (h // 2):LANES * (h // 2) + LANES], h)
          for h in range(ATT_HEADS)]
    mx_ref[...] = jnp.full(mx_ref.shape, NEG_BIG, F32)
    racc_ref[...] = jnp.zeros(racc_ref.shape, F32)

    def logit_body(c, carry):
        r0 = pl.multiple_of(c * kc, kc)
        bias = bias_ref[c]
        for h in range(ATT_HEADS):
            p = h // 2
            kch = k_ref[pl.ds(r0, kc), LANES * p:LANES * p + LANES]
            lg = lax.dot_general(kch, qm[h], _NT, preferred_element_type=F32) + bias
            lg_ref[h, c] = lg
            mx_ref[h] = jnp.maximum(mx_ref[h], _fold_rows(lg, 8, jnp.maximum))
        return carry

    lax.fori_loop(0, nch, logit_body, 0)
    mrow = [jnp.max(mx_ref[h], axis=0, keepdims=True) for h in range(ATT_HEADS)]
    ones_rows = jnp.ones((BF16_ROWS, kc), BF16)

    def pv_body(c, carry):
        for h in range(ATT_HEADS):
            vaug = jnp.concatenate(
                [vt_ref[c, ATT_HEAD_DIM * h:ATT_HEAD_DIM * (h + 1), :], ones_rows], axis=0)
            pr = jnp.exp(lg_ref[h, c] - mrow[h]).astype(BF16)
            racc_ref[h] += jnp.dot(vaug, pr, preferred_element_type=F32)
        return carry

    lax.fori_loop(0, nch, pv_body, 0)

    for h in range(ATT_HEADS):
        r = racc_ref[h]
        ot_ref[ATT_HEAD_DIM * h:ATT_HEAD_DIM * (h + 1), :] = (
            r[0:ATT_HEAD_DIM, :] / r[ATT_HEAD_DIM:ATT_HEAD_DIM + 1, :])
    ya_ref[...] = (ot_ref[...].T * gaz_ref[...]).astype(BF16)


def _attn_call(q, k, vt, iq, ik, iwt, gaz, *, tq, kc, topk):
    B, S, _ = q.shape
    nq = S // tq
    nk = S // kc
    qtile = lambda w: pl.BlockSpec((None, tq, w), lambda b, j: (b, j, 0))
    seq = lambda w: pl.BlockSpec((None, S, w), lambda b, j: (b, 0, 0))
    return pl.pallas_call(
        functools.partial(_attn_kernel, tq=tq, kc=kc, nk=nk, topk=topk),
        grid=(B, nq),
        in_specs=[
            qtile(ATT_WIDTH), seq(ATT_WIDTH),
            pl.BlockSpec((None, nk, VT_ROWS, kc), lambda b, j: (b, 0, 0, 0)),
            qtile(IDX_HEADS * IDX_DIM), seq(2 * IDX_DIM),
            pl.BlockSpec((None, IW_ROWS, tq), lambda b, j: (b, 0, j)),
            qtile(ATT_WIDTH),
        ],
        out_specs=qtile(ATT_WIDTH),
        out_shape=jax.ShapeDtypeStruct((B, S, ATT_WIDTH), BF16),
        scratch_shapes=[
            pltpu.VMEM((nk, kc, tq), I32),
            pltpu.VMEM((nk, KEY_BITS, 8, tq), I32),
            pltpu.VMEM((nk, 8, tq), I32),
            pltpu.VMEM((nk, kc, tq), F32),
            pltpu.VMEM((ATT_HEADS, nk, kc, tq), F32),
            pltpu.VMEM((ATT_HEADS, 8, tq), F32),
            pltpu.VMEM((ATT_HEADS, PV_ROWS, tq), F32),
            pltpu.VMEM((ATT_WIDTH, tq), F32),
        ],
        compiler_params=pltpu.CompilerParams(
            dimension_semantics=("arbitrary", "arbitrary"), vmem_limit_bytes=VMEM_LIMIT),
        name="attn",
    )(q, k, vt, iq, ik, iwt, gaz)


def _mix_kernel(pu_ref, gpz_ref, gq_ref, gk_ref, gv_ref, ggz_ref, gg_ref,
                pw_ref, ps_ref, wg_ref, bg_ref, og_ref,
                yb_ref, yc_ref, pad_ref, qt_ref, kt_ref, kd_ref, dec_ref, sts_ref, *, seq, rb):
    wmax = max(POOL_WINDOWS)
    pad_ref[0:wmax, :] = jnp.zeros((wmax, POOL_WIDTH), F32)
    pad_ref[wmax:wmax + seq, :] = pu_ref[...]
    lane = lax.broadcasted_iota(I32, (rb, POOL_WIDTH), 1)
    grp = lane // POOL_GROUP_DIM
    win = jnp.zeros((rb, POOL_WIDTH), I32)
    for g, w in enumerate(POOL_WINDOWS):
        win = jnp.where(grp == g, w, win)
    for blk in range(seq // rb):
        r0 = blk * rb
        xh = pad_ref[r0:r0 + rb + wmax, :]
        wsum = None
        acc, width = xh, 1
        while width < wmax:
            acc = acc + pltpu.roll(acc, width, axis=0)
            width *= 2
            if width in POOL_WINDOWS:
                cur = acc[wmax:, :]
                wsum = cur if wsum is None else jnp.where(win == width, cur, wsum)
        t1 = r0 + 1 + lax.broadcasted_iota(I32, (rb, POOL_WIDTH), 0)
        cnt = jnp.minimum(t1, win).astype(F32)
        pooled = wsum / cnt - xh[wmax:, :]
        y = jnp.dot(pooled.astype(BF16), pw_ref[...], preferred_element_type=F32)
        y = y * ps_ref[...] * gpz_ref[r0:r0 + rb, :]
        yb_ref[r0:r0 + rb, :] = y.astype(BF16)

    C = GLA_CHUNK
    H = GLA_HEADS
    NB = GLA_BLOCK
    bi = lax.broadcasted_iota(I32, (NB, NB), 0)
    bj = lax.broadcasted_iota(I32, (NB, NB), 1)
    tri_blk = jnp.where((bi >= bj) & (bi // C == bj // C), 1.0, 0.0).astype(F32)

    def gate_body(blk, carry):
        r0 = pl.multiple_of(blk * NB, NB)
        z = jnp.dot(gg_ref[pl.ds(r0, NB), :], wg_ref[...], preferred_element_type=F32,
                    precision=lax.Precision.HIGHEST) + bg_ref[...]
        g = jax.nn.log_sigmoid(z) * (1.0 / GLA_TAU)
        b = jnp.dot(tri_blk, g, preferred_element_type=F32, precision=lax.Precision.HIGHEST)
        for cc in range(NB // C):
            rows = pl.ds(r0 + cc * C, C)
            bc = b[cc * C:(cc + 1) * C, :]
            bend = bc[C - 1:C, :]
            qc = gq_ref[rows, :]
            kcv = gk_ref[rows, :]
            qt_ref[rows, :] = (qc * jnp.exp(bc) * (GLA_DK ** -0.5)).astype(BF16)
            kt_ref[rows, :] = (kcv * jnp.exp(-bc)).astype(BF16)
            kd_ref[rows, :] = (kcv * jnp.exp(bend - bc)).astype(BF16)
            dec_ref[pl.ds(pl.multiple_of((r0 + cc * C) // 8, 8), 8), :] = jnp.broadcast_to(
                jnp.exp(bend), (8, GLA_KEY_PAD))
        return carry

    lax.fori_loop(0, seq // NB, gate_body, 0, unroll=2)

    slane = lax.broadcasted_iota(I32, (GLA_DV_PAD, GLA_KEY_PAD), 1) // GLA_DK_PAD

    def state_body(n, st):
        r0 = pl.multiple_of(n * C, C)
        v = gv_ref[pl.ds(r0, C), :]
        kd = kd_ref[pl.ds(r0, C), :]
        upd = lax.dot_general(v, kd, _TN, preferred_element_type=F32)
        sts_ref[n] = st.astype(BF16)
        new = dec_ref[pl.ds(pl.multiple_of(n * 8, 8), 1), :] * st
        for h in range(H):
            new = new + jnp.where(slane == h, upd[h * GLA_DV_PAD:(h + 1) * GLA_DV_PAD, :], 0.0)
        return new

    lax.fori_loop(0, seq // C, state_body, jnp.zeros((GLA_DV_PAD, GLA_KEY_PAD), F32), unroll=4)

    ri = lax.broadcasted_iota(I32, (C, C), 0)
    ci = lax.broadcasted_iota(I32, (C, C), 1)
    klane = lax.broadcasted_iota(I32, (C, GLA_KEY_PAD), 1) // GLA_DK_PAD

    def out_body(n, carry):
        r0 = pl.multiple_of(n * C, C)
        qt = qt_ref[pl.ds(r0, C), :]
        v = gv_ref[pl.ds(r0, C), :]
        qs = jnp.concatenate(
            [jnp.where(klane == h, qt, jnp.zeros_like(qt)) for h in range(H)], axis=0)
        a = lax.dot_general(qs, kt_ref[pl.ds(r0, C), :], _NT, preferred_element_type=F32)
        o_inter = lax.dot_general(qs, sts_ref[n], _NT, preferred_element_type=F32)
        for h in range(H):
            ah = jnp.where(ri >= ci, a[h * C:(h + 1) * C, :], 0.0).astype(BF16)
            vh = v[:, h * GLA_DV_PAD:(h + 1) * GLA_DV_PAD]
            o = jnp.dot(ah, vh, preferred_element_type=F32) + o_inter[h * C:(h + 1) * C, :]
            ms = jnp.sum(o * o, axis=-1, keepdims=True) * (1.0 / GLA_DV)
            y = o * lax.rsqrt(ms + EPS) * og_ref[...]
            y = y * ggz_ref[pl.ds(r0, C), h * GLA_DV_PAD:(h + 1) * GLA_DV_PAD]
            yc_ref[pl.ds(r0, C), h * GLA_DV_PAD:(h + 1) * GLA_DV_PAD] = y.astype(BF16)
        return carry

    lax.fori_loop(0, seq // C, out_body, 0, unroll=4)


def _mix_call(pu, gpz, gq, gk, gv, ggz, gg, pw, ps, wg, bg, og):
    B, S, _ = pu.shape
    rb = min(256, S)
    seqb = lambda w: pl.BlockSpec((None, S, w), lambda b: (b, 0, 0))
    full = lambda a: pl.BlockSpec(a.shape, lambda b: (0,) * a.ndim)
    return pl.pallas_call(
        functools.partial(_mix_kernel, seq=S, rb=rb),
        grid=(B,),
        in_specs=[seqb(POOL_WIDTH), seqb(POOL_WIDTH), seqb(GLA_KEY_PAD), seqb(GLA_KEY_PAD),
                  seqb(GLA_VAL_PAD), seqb(GLA_VAL_PAD), seqb(GATE_PAD),
                  full(pw), full(ps), full(wg), full(bg), full(og)],
        out_specs=(seqb(POOL_WIDTH), seqb(GLA_VAL_PAD)),
        out_shape=(jax.ShapeDtypeStruct((B, S, POOL_WIDTH), BF16),
                   jax.ShapeDtypeStruct((B, S, GLA_VAL_PAD), BF16)),
        scratch_shapes=[
            pltpu.VMEM((max(POOL_WINDOWS) + S, POOL_WIDTH), F32),
            pltpu.VMEM((S, GLA_KEY_PAD), BF16),
            pltpu.VMEM((S, GLA_KEY_PAD), BF16),
            pltpu.VMEM((S, GLA_KEY_PAD), BF16),
            pltpu.VMEM((S // GLA_CHUNK * 8, GLA_KEY_PAD), F32),
            pltpu.VMEM((S // GLA_CHUNK, GLA_DV_PAD, GLA_KEY_PAD), BF16),
        ],
        compiler_params=pltpu.CompilerParams(
            dimension_semantics=("parallel",), vmem_limit_bytes=VMEM_LIMIT),
        name="mix",
    )(pu, gpz, gq, gk, gv, ggz, gg, pw, ps, wg, bg, og)


def _out_kernel(x_ref, ya_ref, yb_ref, yc_ref, wa_ref, wb_ref, wc_ref, o_ref):
    y = jnp.dot(ya_ref[...], wa_ref[...], preferred_element_type=F32)
    y = y + jnp.dot(yb_ref[...], wb_ref[...], preferred_element_type=F32)
    y = y + jnp.dot(yc_ref[...], wc_ref[...], preferred_element_type=F32)
    o_ref[...] = x_ref[...] + y


def _out_call(x2, ya, yb, yc, wa, wb, wc, *, tm):
    T = x2.shape[0]
    row = lambda w: pl.BlockSpec((tm, w), lambda i: (i, 0))
    full = lambda a: pl.BlockSpec(a.shape, lambda i: (0,) * a.ndim)
    return pl.pallas_call(
        _out_kernel,
        grid=(T // tm,),
        in_specs=[row(D_MODEL), row(ATT_WIDTH), row(POOL_WIDTH), row(GLA_VAL_PAD),
                  full(wa), full(wb), full(wc)],
        out_specs=row(D_MODEL),
        out_shape=jax.ShapeDtypeStruct((T, D_MODEL), F32),
        compiler_params=pltpu.CompilerParams(
            dimension_semantics=("parallel",), vmem_limit_bytes=VMEM_LIMIT),
        name="outproj",
    )(x2, ya, yb, yc, wa, wb, wc)


def _pad_heads(w, heads, d, dpad, axis):
    shp = list(w.shape)
    w = w.reshape(shp[:axis] + [heads, d] + shp[axis + 1:])
    pad = [(0, 0)] * w.ndim
    pad[axis + 1] = (0, dpad - d)
    w = jnp.pad(w, pad)
    return w.reshape(shp[:axis] + [heads * dpad] + shp[axis + 1:])


def _pack_layer(norm_g, w_in, att_q_gain, att_k_gain, pool_w, pool_scale,
                gla_w_gate, gla_b_gate, gla_out_gain, w_out):
    offs = [0]
    for s in IN_SPLITS:
        offs.append(offs[-1] + s)
    (aq, ak, av, az, iq, ik, iw, pu, pz, gq, gk, gv, gg, gz) = [
        w_in[:, offs[i]:offs[i + 1]] for i in range(len(IN_SPLITS))]
    nat = {
        "q": aq, "k": ak, "z": az, "iq": iq,
        "ik2": jnp.concatenate([ik, ik], axis=1),
        "pu": pu, "pz": pz,
        "gq": _pad_heads(gq, GLA_HEADS, GLA_DK, GLA_DK_PAD, 1),
        "gk": _pad_heads(gk, GLA_HEADS, GLA_DK, GLA_DK_PAD, 1),
        "gv": _pad_heads(gv, GLA_HEADS, GLA_DV, GLA_DV_PAD, 1),
        "gz": _pad_heads(gz, GLA_HEADS, GLA_DV, GLA_DV_PAD, 1),
        "gg": jnp.pad(gg, ((0, 0), (0, GATE_PAD - GLA_GATE_RANK))),
    }
    wn = jnp.concatenate([nat[name] for name, _ in _NAT_PIECES], axis=1).astype(BF16)
    iw_t = jnp.pad(iw.T * (IDX_HEADS ** -0.5), ((0, IW_ROWS - IDX_HEADS), (0, 0)))
    wt = jnp.concatenate([av.T, iw_t], axis=0).astype(BF16)
    qg = (jnp.tile(att_q_gain, ATT_HEADS) * (ATT_HEAD_DIM ** -0.5)).reshape(1, ATT_WIDTH)
    kg = jnp.tile(att_k_gain, ATT_HEADS).reshape(1, ATT_WIDTH)
    hid = jnp.arange(ATT_WIDTH) // ATT_HEAD_DIM
    gm = jnp.where(hid[:, None] == hid[None, :], 1.0 / ATT_HEAD_DIM, 0.0).astype(BF16)
    pw = jax.scipy.linalg.block_diag(*[pool_w[g] for g in range(POOL_GROUPS)]).astype(BF16)
    ps = pool_scale.reshape(1, POOL_WIDTH)
    wg = jnp.pad(_pad_heads(gla_w_gate, GLA_HEADS, GLA_DK, GLA_DK_PAD, 1),
                 ((0, GATE_PAD - GLA_GATE_RANK), (0, 0)))
    bg = _pad_heads(gla_b_gate.reshape(1, -1), GLA_HEADS, GLA_DK, GLA_DK_PAD, 1)
    og = jnp.pad(gla_out_gain, (0, GLA_DV_PAD - GLA_DV)).reshape(1, GLA_DV_PAD)
    wa = w_out[0:ATT_WIDTH].astype(BF16)
    wb = w_out[ATT_WIDTH:ATT_WIDTH + POOL_WIDTH].astype(BF16)
    wc = _pad_heads(w_out[ATT_WIDTH + POOL_WIDTH:], GLA_HEADS, GLA_DV, GLA_DV_PAD, 0).astype(BF16)
    return dict(ng=norm_g.reshape(1, D_MODEL), wn=wn, wt=wt, qg=qg, kg=kg, gm=gm,
                pw=pw, ps=ps, wg=wg, bg=bg, og=og, wa=wa, wb=wb, wc=wc)


def _layer(x2, p, *, B, S, tm, tq, kc, topk):
    (q, k, gaz, iq, ik, pu, gpz, gq, gk, gv, ggz, gg, vt, iwt) = _proj_call(
        x2, p["ng"], p["wn"], p["wt"], p["qg"], p["kg"], p["gm"], tm=tm, kc=kc)
    r3 = lambda a: a.reshape(B, S, a.shape[-1])
    ya = _attn_call(r3(q), r3(k), vt.reshape(B, S // kc, VT_ROWS, kc), r3(iq), r3(ik),
                    iwt.reshape(IW_ROWS, B, S).transpose(1, 0, 2), r3(gaz), tq=tq, kc=kc, topk=topk)
    yb, yc = _mix_call(r3(pu), r3(gpz), r3(gq), r3(gk), r3(gv), r3(ggz), r3(gg),
                       p["pw"], p["ps"], p["wg"], p["bg"], p["og"])
    T = B * S
    return _out_call(x2, ya.reshape(T, ATT_WIDTH), yb.reshape(T, POOL_WIDTH),
                     yc.reshape(T, GLA_VAL_PAD), p["wa"], p["wb"], p["wc"], tm=tm)


def _tiles(S):
    tq = min(512, S // 2)
    kc = min(256, tq)
    tm = min(512, S)
    return tm, tq, kc


def kernel(x, norm_g, w_in, att_q_gain, att_k_gain, pool_w, pool_scale, gla_w_gate,
           gla_b_gate, gla_out_gain, w_out):
    B, S, _ = x.shape
    depth = norm_g.shape[0]
    tm, tq, kc = _tiles(S)
    topk = min(TOPK_MAX, S // 4)
    x2 = x.reshape(B * S, D_MODEL)
    for l in range(depth):
        p = _pack_layer(norm_g[l], w_in[l], att_q_gain[l], att_k_gain[l], pool_w[l],
                        pool_scale[l], gla_w_gate[l], gla_b_gate[l], gla_out_gain[l], w_out[l])
        x2 = _layer(x2, p, B=B, S=S, tm=tm, tq=tq, kc=kc, topk=topk)
    return x2.reshape(B, S, D_MODEL)
```

```python
import functools

import jax
import jax.numpy as jnp
from jax import lax
from jax.experimental import pallas as pl
from jax.experimental.pallas import tpu as pltpu

F32 = jnp.float32
BF16 = jnp.bfloat16
I32 = jnp.int32

D_MODEL = 1024
ATT_HEADS = 6
ATT_HEAD_DIM = 64
ATT_WIDTH = ATT_HEADS * ATT_HEAD_DIM
IDX_HEADS = 4
IDX_DIM = 64
TOPK_MAX = 256
POOL_GROUPS = 4
POOL_GROUP_DIM = 64
POOL_WIDTH = POOL_GROUPS * POOL_GROUP_DIM
POOL_WINDOWS = (2, 4, 8, 16)
GLA_HEADS = 4
GLA_DK = 48
GLA_DV = 96
GLA_KEY_WIDTH = GLA_HEADS * GLA_DK
GLA_VAL_WIDTH = GLA_HEADS * GLA_DV
GLA_GATE_RANK = 16
GLA_TAU = 16.0
GLA_CHUNK = 64
GLA_BLOCK = 256
EPS = 1e-6

IN_SPLITS = (
    ATT_WIDTH, ATT_WIDTH, ATT_WIDTH, ATT_WIDTH,
    IDX_HEADS * IDX_DIM, IDX_DIM, IDX_HEADS,
    POOL_WIDTH, POOL_WIDTH,
    GLA_KEY_WIDTH, GLA_KEY_WIDTH, GLA_VAL_WIDTH,
    GLA_GATE_RANK, GLA_VAL_WIDTH,
)

LANES = 128
BF16_ROWS = 16
GLA_DK_PAD = 64
GLA_DV_PAD = 128
GLA_KEY_PAD = GLA_HEADS * GLA_DK_PAD
GLA_VAL_PAD = GLA_HEADS * GLA_DV_PAD
GATE_PAD = LANES

_NAT_PIECES = (
    ("q", ATT_WIDTH), ("k", ATT_WIDTH), ("z", ATT_WIDTH),
    ("iq", IDX_HEADS * IDX_DIM), ("ik2", 2 * IDX_DIM),
    ("pu", POOL_WIDTH), ("pz", POOL_WIDTH),
    ("gq", GLA_KEY_PAD), ("gk", GLA_KEY_PAD), ("gv", GLA_VAL_PAD),
    ("gz", GLA_VAL_PAD), ("gg", GATE_PAD),
)
_NAT_OFF = {}
_off = 0
for _name, _w in _NAT_PIECES:
    _NAT_OFF[_name] = (_off, _off + _w)
    _off += _w
NAT_WIDTH = _off
_NAT_GROUPS = (("q", "k", "z", "iq", "ik2"), ("pu", "pz"), ("gq", "gk"), ("gv", "gz"), ("gg",))
VT_ROWS = ATT_WIDTH
IW_ROWS = 8
TRN_ROWS = VT_ROWS + IW_ROWS
PV_ROWS = ATT_HEAD_DIM + BF16_ROWS

INT_MIN = -(2 ** 31)
KEY_BITS = 32
_TRANSPOSE_MASKS = {16: 0x0000FFFF, 8: 0x00FF00FF, 4: 0x0F0F0F0F, 2: 0x33333333, 1: 0x55555555}
NEG_BIG = -1e30
VMEM_LIMIT = 56 * 1024 * 1024

_NT = (((1,), (1,)), ((), ()))
_TN = (((0,), (0,)), ((), ()))


def _silu(x):
    return x * jax.nn.sigmoid(x)


def _fold_rows(x, rows, op):
    while x.shape[0] > rows:
        half = x.shape[0] // 2
        x = op(x[:half], x[half:])
    return x


def _proj_kernel(x_ref, ng_ref, wn_ref, wt_ref, qg_ref, kg_ref, gm_ref,
                 q_ref, k_ref, gaz_ref, iq_ref, ik_ref, pu_ref, gpz_ref,
                 gq_ref, gk_ref, gv_ref, ggz_ref, gg_ref, vt_ref, iwt_ref, *, kc):
    x = x_ref[...]
    ms = jnp.mean(x * x, axis=-1, keepdims=True)
    h = (x * lax.rsqrt(ms + EPS) * ng_ref[...]).astype(BF16)

    groups = {}
    for names in _NAT_GROUPS:
        a, b = _NAT_OFF[names[0]][0], _NAT_OFF[names[-1]][1]
        res = jnp.dot(h, wn_ref[:, a:b], preferred_element_type=F32)
        for name in names:
            lo, hi = _NAT_OFF[name]
            groups[name] = res[:, lo - a:hi - a]

    def nat(name):
        return groups[name]

    def head_norm(a, gain):
        hms = jnp.dot((a * a).astype(BF16), gm_ref[...], preferred_element_type=F32)
        return a * lax.rsqrt(hms + EPS) * gain

    q_ref[...] = head_norm(nat("q"), qg_ref[...]).astype(BF16)
    k_ref[...] = head_norm(nat("k"), kg_ref[...]).astype(BF16)
    gaz_ref[...] = _silu(nat("z"))
    iq_ref[...] = nat("iq").astype(BF16)
    ik_ref[...] = nat("ik2").astype(BF16)
    pu_ref[...] = nat("pu")
    gpz_ref[...] = _silu(nat("pz"))
    gq_ref[...] = nat("gq")
    gk_ref[...] = nat("gk")
    gv_ref[...] = nat("gv").astype(BF16)
    ggz_ref[...] = _silu(nat("gz"))
    gg_ref[...] = nat("gg")

    pt = lax.dot_general(wt_ref[...], h, _NT, preferred_element_type=F32)
    tm = x.shape[0]
    for a in range(tm // kc):
        vt_ref[a] = pt[0:VT_ROWS, a * kc:(a + 1) * kc].astype(BF16)
    iwt_ref[...] = pt[VT_ROWS:TRN_ROWS, :]


def _proj_call(x2, ng, wn, wt, qg, kg, gm, *, tm, kc):
    T = x2.shape[0]
    grid = (T // tm,)
    row = lambda w: pl.BlockSpec((tm, w), lambda i: (i, 0))
    full = lambda a: pl.BlockSpec(a.shape, lambda i: (0,) * a.ndim)
    widths_dtypes = (
        (ATT_WIDTH, BF16), (ATT_WIDTH, BF16),
        (ATT_WIDTH, F32),
        (IDX_HEADS * IDX_DIM, BF16), (2 * IDX_DIM, BF16),
        (POOL_WIDTH, F32), (POOL_WIDTH, F32),
        (GLA_KEY_PAD, F32), (GLA_KEY_PAD, F32),
        (GLA_VAL_PAD, BF16), (GLA_VAL_PAD, F32),
        (GATE_PAD, F32),
    )
    out_shape = tuple(jax.ShapeDtypeStruct((T, w), dt) for w, dt in widths_dtypes) + (
        jax.ShapeDtypeStruct((T // kc, VT_ROWS, kc), BF16),
        jax.ShapeDtypeStruct((IW_ROWS, T), F32),
    )
    out_specs = tuple(row(w) for w, _ in widths_dtypes) + (
        pl.BlockSpec((tm // kc, VT_ROWS, kc), lambda i: (i, 0, 0)),
        pl.BlockSpec((IW_ROWS, tm), lambda i: (0, i)),
    )
    return pl.pallas_call(
        functools.partial(_proj_kernel, kc=kc),
        grid=grid,
        in_specs=[row(D_MODEL), full(ng), full(wn), full(wt), full(qg), full(kg), full(gm)],
        out_specs=out_specs,
        out_shape=out_shape,
        compiler_params=pltpu.CompilerParams(
            dimension_semantics=("parallel",), vmem_limit_bytes=VMEM_LIMIT),
        name="proj",
    )(x2, ng, wn, wt, qg, kg, gm)


def _attn_kernel(q_ref, k_ref, vt_ref, iq_ref, ik_ref, iwt_ref, gaz_ref, ya_ref,
                 k32_ref, planes_ref, alive_ref, bias_ref, lg_ref, mx_ref, racc_ref, ot_ref,
                 *, tq, kc, nk, topk):
    j = pl.program_id(1)
    ndiag = tq // kc
    nfull = j * ndiag
    nch = nfull + ndiag
    lane = lax.broadcasted_iota(I32, (tq, LANES), 1)
    lo_half = lane < 64
    row_i = lax.broadcasted_iota(I32, (kc, tq), 0)
    col_i = lax.broadcasted_iota(I32, (kc, tq), 1)

    def half_mask(slab, h):
        keep = lo_half if h % 2 == 0 else jnp.logical_not(lo_half)
        return jnp.where(keep, slab, jnp.zeros_like(slab))

    @pl.when((pl.program_id(0) == 0) & (j == 0))
    def _():
        planes_ref[...] = jnp.zeros(planes_ref.shape, I32)

    iqm = [half_mask(iq_ref[:, LANES * (h // 2):LANES * (h // 2) + LANES], h)
           for h in range(IDX_HEADS)]
    iw = iwt_ref[...]

    def key_chunk(c, causal_mask):
        r0 = pl.multiple_of(c * kc, kc)
        ikc = ik_ref[pl.ds(r0, kc), :]
        acc = jnp.zeros((kc, tq), F32)
        for h in range(IDX_HEADS):
            d = lax.dot_general(ikc, iqm[h], _NT, preferred_element_type=F32)
            acc = acc + jnp.maximum(d, 0.0) * iw[h:h + 1, :]
        bits = lax.bitcast_convert_type(acc, I32)
        key = bits ^ ((bits >> 31) & 0x7FFFFFFF)
        if causal_mask:
            key = jnp.where(r0 + row_i <= j * tq + col_i, key, INT_MIN)
        k32_ref[c] = key
        u = key ^ INT_MIN
        xs = [u[8 * v:8 * v + 8, :] for v in range(KEY_BITS)]
        sh = KEY_BITS // 2
        while sh:
            msk = _TRANSPOSE_MASKS[sh]
            for a in range(KEY_BITS):
                if a & sh == 0:
                    t = (xs[a] ^ lax.shift_right_logical(xs[a + sh], sh)) & msk
                    xs[a] = xs[a] ^ t
                    xs[a + sh] = xs[a + sh] ^ (t << sh)
            sh //= 2
        for i in range(KEY_BITS):
            planes_ref[c, i] = xs[i]

    def key_body(i, carry):
        for dd in range(ndiag):
            key_chunk(i * ndiag + dd, False)
        return carry

    lax.fori_loop(0, j, key_body, 0)
    for dd in range(ndiag):
        key_chunk(nfull + dd, True)

    for c in range(nk):
        alive_ref[c] = jnp.where(c < nch, jnp.full((8, tq), -1, I32), jnp.zeros((8, tq), I32))

    def radix_body(i, carry):
        need_i, thr_u = carry
        cnt = jnp.zeros((8, tq), I32)
        for c in range(nk):
            cnt = cnt + lax.population_count(alive_ref[c] & planes_ref[c, i])
        c1 = cnt.sum(axis=0, keepdims=True)
        take1 = c1 >= need_i
        for c in range(nk):
            al = alive_ref[c]
            ones = al & planes_ref[c, i]
            alive_ref[c] = jnp.where(take1, ones, al ^ ones)
        bit = lax.shift_right_logical(jnp.int32(INT_MIN), i)
        return jnp.where(take1, need_i, need_i - c1), jnp.where(take1, thr_u | bit, thr_u)

    need_i, thr_u = lax.fori_loop(
        0, KEY_BITS, radix_body, (jnp.full((1, tq), topk, I32), jnp.zeros((1, tq), I32)))
    thr = thr_u ^ INT_MIN
    n_eq = jnp.zeros((8, tq), I32)
    for c in range(nk):
        n_eq = n_eq + lax.population_count(alive_ref[c])
    n_eq = n_eq.sum(axis=0, keepdims=True)
    need = jnp.where(thr == INT_MIN, 0, need_i).astype(F32)
    excess = jnp.where(thr == INT_MIN, 0, n_eq - need_i)
    has_excess_ties = jnp.max(excess) > 0

    def bias_with_ties():
        tri = jnp.where(lax.broadcasted_iota(I32, (kc, kc), 0) >= lax.broadcasted_iota(I32, (kc, kc), 1),
                        1.0, 0.0).astype(BF16)

        def body(c, run):
            u = k32_ref[c]
            tie = u == thr
            tie_f = jnp.where(tie, 1.0, 0.0).astype(BF16)
            rank = run + jnp.dot(tri, tie_f, preferred_element_type=F32)
            take_tie = jnp.where(tie, jnp.where(rank <= need, 0.0, NEG_BIG), NEG_BIG)
            bias_ref[c] = jnp.where(u > thr, 0.0, take_tie)
            return rank[kc - 1:kc, :]

        lax.fori_loop(0, nch, body, jnp.zeros((1, tq), F32))

    def bias_no_ties():
        thr_eff = jnp.maximum(thr, INT_MIN + 1)

        def body(c, carry):
            bias_ref[c] = jnp.where(k32_ref[c] >= thr_eff, 0.0, NEG_BIG)
            return carry

        lax.fori_loop(0, nch, body, 0)

    lax.cond(has_excess_ties, bias_with_ties, bias_no_ties)

    qm = [half_mask(q_ref[:, LANES * (h // 2):LANES * (h // 2) + LANES], h)
          for h in range(ATT_HEADS)]
    mx_ref[...] = jnp.full(mx_ref.shape, NEG_BIG, F32)
    racc_ref[...] = jnp.zeros(racc_ref.shape, F32)

    def logit_body(i, carry):
        for dd in range(ndiag):
            c = i * ndiag + dd
            r0 = pl.multiple_of(c * kc, kc)
            bias = bias_ref[c]
            for h in range(ATT_HEADS):
                p = h // 2
                kch = k_ref[pl.ds(r0, kc), LANES * p:LANES * p + LANES]
                lg = lax.dot_general(kch, qm[h], _NT, preferred_element_type=F32) + bias
                lg_ref[h, c] = lg
                mx_ref[h] = jnp.maximum(mx_ref[h], _fold_rows(lg, 8, jnp.maximum))
        return carry

    lax.fori_loop(0, j + 1, logit_body, 0)
    mrow = [jnp.max(mx_ref[h], axis=0, keepdims=True) for h in range(ATT_HEADS)]
    ones_rows = jnp.ones((BF16_ROWS, kc), BF16)

    def pv_body(i, carry):
        for dd in range(ndiag):
            c = i * ndiag + dd
            for h in range(ATT_HEADS):
                vaug = jnp.concatenate(
                    [vt_ref[c, ATT_HEAD_DIM * h:ATT_HEAD_DIM * (h + 1), :], ones_rows], axis=0)
                pr = jnp.exp(lg_ref[h, c] - mrow[h]).astype(BF16)
                racc_ref[h] += jnp.dot(vaug, pr, preferred_element_type=F32)
        return carry

    lax.fori_loop(0, j + 1, pv_body, 0)

    for h in range(ATT_HEADS):
        r = racc_ref[h]
        ot_ref[ATT_HEAD_DIM * h:ATT_HEAD_DIM * (h + 1), :] = (
            r[0:ATT_HEAD_DIM, :] / r[ATT_HEAD_DIM:ATT_HEAD_DIM + 1, :])
    ya_ref[...] = (ot_ref[...].T * gaz_ref[...]).astype(BF16)


def _attn_call(q, k, vt, iq, ik, iwt, gaz, *, tq, kc, topk):
    B, S, _ = q.shape
    nq = S // tq
    nk = S // kc
    qtile = lambda w: pl.BlockSpec((None, tq, w), lambda b, j: (b, j, 0))
    seq = lambda w: pl.BlockSpec((None, S, w), lambda b, j: (b, 0, 0))
    return pl.pallas_call(
        functools.partial(_attn_kernel, tq=tq, kc=kc, nk=nk, topk=topk),
        grid=(B, nq),
        in_specs=[
            qtile(ATT_WIDTH), seq(ATT_WIDTH),
            pl.BlockSpec((None, nk, VT_ROWS, kc), lambda b, j: (b, 0, 0, 0)),
            qtile(IDX_HEADS * IDX_DIM), seq(2 * IDX_DIM),
            pl.BlockSpec((None, IW_ROWS, tq), lambda b, j: (b, 0, j)),
            qtile(ATT_WIDTH),
        ],
        out_specs=qtile(ATT_WIDTH),
        out_shape=jax.ShapeDtypeStruct((B, S, ATT_WIDTH), BF16),
        scratch_shapes=[
            pltpu.VMEM((nk, kc, tq), I32),
            pltpu.VMEM((nk, KEY_BITS, 8, tq), I32),
            pltpu.VMEM((nk, 8, tq), I32),
            pltpu.VMEM((nk, kc, tq), F32),
            pltpu.VMEM((ATT_HEADS, nk, kc, tq), F32),
            pltpu.VMEM((ATT_HEADS, 8, tq), F32),
            pltpu.VMEM((ATT_HEADS, PV_ROWS, tq), F32),
            pltpu.VMEM((ATT_WIDTH, tq), F32),
        ],
        compiler_params=pltpu.CompilerParams(
            dimension_semantics=("arbitrary", "arbitrary"), vmem_limit_bytes=VMEM_LIMIT),
        name="attn",
    )(q, k, vt, iq, ik, iwt, gaz)


def _mix_kernel(pu_ref, gpz_ref, gq_ref, gk_ref, gv_ref, ggz_ref, gg_ref,
                pw_ref, ps_ref, wg_ref, bg_ref, og_ref,
                yb_ref, yc_ref, pad_ref, qt_ref, kt_ref, kd_ref, dec_ref, sts_ref, *, seq, rb):
    wmax = max(POOL_WINDOWS)
    pad_ref[0:wmax, :] = jnp.zeros((wmax, POOL_WIDTH), F32)
    pad_ref[wmax:wmax + seq, :] = pu_ref[...]
    lane = lax.broadcasted_iota(I32, (rb, POOL_WIDTH), 1)
    grp = lane // POOL_GROUP_DIM
    win = jnp.zeros((rb, POOL_WIDTH), I32)
    for g, w in enumerate(POOL_WINDOWS):
        win = jnp.where(grp == g, w, win)
    for blk in range(seq // rb):
        r0 = blk * rb
        xh = pad_ref[r0:r0 + rb + wmax, :]
        wsum = None
        acc, width = xh, 1
        while width < wmax:
            acc = acc + pltpu.roll(acc, width, axis=0)
            width *= 2
            if width in POOL_WINDOWS:
                cur = acc[wmax:, :]
                wsum = cur if wsum is None else jnp.where(win == width, cur, wsum)
        t1 = r0 + 1 + lax.broadcasted_iota(I32, (rb, POOL_WIDTH), 0)
        cnt = jnp.minimum(t1, win).astype(F32)
        pooled = wsum / cnt - xh[wmax:, :]
        y = jnp.dot(pooled.astype(BF16), pw_ref[...], preferred_element_type=F32)
        y = y * ps_ref[...] * gpz_ref[r0:r0 + rb, :]
        yb_ref[r0:r0 + rb, :] = y.astype(BF16)

    C = GLA_CHUNK
    H = GLA_HEADS
    NB = GLA_BLOCK
    bi = lax.broadcasted_iota(I32, (NB, NB), 0)
    bj = lax.broadcasted_iota(I32, (NB, NB), 1)
    tri_blk = jnp.where((bi >= bj) & (bi // C == bj // C), 1.0, 0.0).astype(F32)

    def gate_body(blk, carry):
        r0 = pl.multiple_of(blk * NB, NB)
        z = jnp.dot(gg_ref[pl.ds(r0, NB), :], wg_ref[...], preferred_element_type=F32,
                    precision=lax.Precision.HIGHEST) + bg_ref[...]
        g = jax.nn.log_sigmoid(z) * (1.0 / GLA_TAU)
        b = jnp.dot(tri_blk, g, preferred_element_type=F32, precision=lax.Precision.HIGHEST)
        for cc in range(NB // C):
            rows = pl.ds(r0 + cc * C, C)
            bc = b[cc * C:(cc + 1) * C, :]
            bend = bc[C - 1:C, :]
            qc = gq_ref[rows, :]
            kcv = gk_ref[rows, :]
            qt_ref[rows, :] = (qc * jnp.exp(bc) * (GLA_DK ** -0.5)).astype(BF16)
            kt_ref[rows, :] = (kcv * jnp.exp(-bc)).astype(BF16)
            kd_ref[rows, :] = (kcv * jnp.exp(bend - bc)).astype(BF16)
            dec_ref[pl.ds(pl.multiple_of((r0 + cc * C) // 8, 8), 8), :] = jnp.broadcast_to(
                jnp.exp(bend), (8, GLA_KEY_PAD))
        return carry

    lax.fori_loop(0, seq // NB, gate_body, 0, unroll=2)

    slane = lax.broadcasted_iota(I32, (GLA_DV_PAD, GLA_KEY_PAD), 1) // GLA_DK_PAD

    def state_body(n, st):
        r0 = pl.multiple_of(n * C, C)
        v = gv_ref[pl.ds(r0, C), :]
        kd = kd_ref[pl.ds(r0, C), :]
        upd = lax.dot_general(v, kd, _TN, preferred_element_type=F32)
        sts_ref[n] = st.astype(BF16)
        new = dec_ref[pl.ds(pl.multiple_of(n * 8, 8), 1), :] * st
        for h in range(H):
            new = new + jnp.where(slane == h, upd[h * GLA_DV_PAD:(h + 1) * GLA_DV_PAD, :], 0.0)
        return new

    lax.fori_loop(0, seq // C, state_body, jnp.zeros((GLA_DV_PAD, GLA_KEY_PAD), F32), unroll=4)

    ri = lax.broadcasted_iota(I32, (C, C), 0)
    ci = lax.broadcasted_iota(I32, (C, C), 1)
    klane = lax.broadcasted_iota(I32, (C, GLA_KEY_PAD), 1) // GLA_DK_PAD

    def out_body(n, carry):
        r0 = pl.multiple_of(n * C, C)
        qt = qt_ref[pl.ds(r0, C), :]
        v = gv_ref[pl.ds(r0, C), :]
        qs = jnp.concatenate(
            [jnp.where(klane == h, qt, jnp.zeros_like(qt)) for h in range(H)], axis=0)
        a = lax.dot_general(qs, kt_ref[pl.ds(r0, C), :], _NT, preferred_element_type=F32)
        o_inter = lax.dot_general(qs, sts_ref[n], _NT, preferred_element_type=F32)
        for h in range(H):
            ah = jnp.where(ri >= ci, a[h * C:(h + 1) * C, :], 0.0).astype(BF16)
            vh = v[:, h * GLA_DV_PAD:(h + 1) * GLA_DV_PAD]
            o = jnp.dot(ah, vh, preferred_element_type=F32) + o_inter[h * C:(h + 1) * C, :]
            ms = jnp.sum(o * o, axis=-1, keepdims=True) * (1.0 / GLA_DV)
            y = o * lax.rsqrt(ms + EPS) * og_ref[...]
            y = y * ggz_ref[pl.ds(r0, C), h * GLA_DV_PAD:(h + 1) * GLA_DV_PAD]
            yc_ref[pl.ds(r0, C), h * GLA_DV_PAD:(h + 1) * GLA_DV_PAD] = y.astype(BF16)
        return carry

    lax.fori_loop(0, seq // C, out_body, 0, unroll=4)


def _mix_call(pu, gpz, gq, gk, gv, ggz, gg, pw, ps, wg, bg, og):
    B, S, _ = pu.shape
    rb = min(256, S)
    seqb = lambda w: pl.BlockSpec((None, S, w), lambda b: (b, 0, 0))
    full = lambda a: pl.BlockSpec(a.shape, lambda b: (0,) * a.ndim)
    return pl.pallas_call(
        functools.partial(_mix_kernel, seq=S, rb=rb),
        grid=(B,),
        in_specs=[seqb(POOL_WIDTH), seqb(POOL_WIDTH), seqb(GLA_KEY_PAD), seqb(GLA_KEY_PAD),
                  seqb(GLA_VAL_PAD), seqb(GLA_VAL_PAD), seqb(GATE_PAD),
                  full(pw), full(ps), full(wg), full(bg), full(og)],
        out_specs=(seqb(POOL_WIDTH), seqb(GLA_VAL_PAD)),
        out_shape=(jax.ShapeDtypeStruct((B, S, POOL_WIDTH), BF16),
                   jax.ShapeDtypeStruct((B, S, GLA_VAL_PAD), BF16)),
        scratch_shapes=[
            pltpu.VMEM((max(POOL_WINDOWS) + S, POOL_WIDTH), F32),
            pltpu.VMEM((S, GLA_KEY_PAD), BF16),
            pltpu.VMEM((S, GLA_KEY_PAD), BF16),
            pltpu.VMEM((S, GLA_KEY_PAD), BF16),
            pltpu.VMEM((S // GLA_CHUNK * 8, GLA_KEY_PAD), F32),
            pltpu.VMEM((S // GLA_CHUNK, GLA_DV_PAD, GLA_KEY_PAD), BF16),
        ],
        compiler_params=pltpu.CompilerParams(
            dimension_semantics=("parallel",), vmem_limit_bytes=VMEM_LIMIT),
        name="mix",
    )(pu, gpz, gq, gk, gv, ggz, gg, pw, ps, wg, bg, og)


def _out_kernel(x_ref, ya_ref, yb_ref, yc_ref, wa_ref, wb_ref, wc_ref, o_ref):
    y = jnp.dot(ya_ref[...], wa_ref[...], preferred_element_type=F32)
    y = y + jnp.dot(yb_ref[...], wb_ref[...], preferred_element_type=F32)
    y = y + jnp.dot(yc_ref[...], wc_ref[...], preferred_element_type=F32)
    o_ref[...] = x_ref[...] + y


def _out_call(x2, ya, yb, yc, wa, wb, wc, *, tm):
    T = x2.shape[0]
    row = lambda w: pl.BlockSpec((tm, w), lambda i: (i, 0))
    full = lambda a: pl.BlockSpec(a.shape, lambda i: (0,) * a.ndim)
    return pl.pallas_call(
        _out_kernel,
        grid=(T // tm,),
        in_specs=[row(D_MODEL), row(ATT_WIDTH), row(POOL_WIDTH), row(GLA_VAL_PAD),
                  full(wa), full(wb), full(wc)],
        out_specs=row(D_MODEL),
        out_shape=jax.ShapeDtypeStruct((T, D_MODEL), F32),
        compiler_params=pltpu.CompilerParams(
            dimension_semantics=("parallel",), vmem_limit_bytes=VMEM_LIMIT),
        name="outproj",
    )(x2, ya, yb, yc, wa, wb, wc)


def _pad_heads(w, heads, d, dpad, axis):
    shp = list(w.shape)
    w = w.reshape(shp[:axis] + [heads, d] + shp[axis + 1:])
    pad = [(0, 0)] * w.ndim
    pad[axis + 1] = (0, dpad - d)
    w = jnp.pad(w, pad)
    return w.reshape(shp[:axis] + [heads * dpad] + shp[axis + 1:])


def _pack_layer(norm_g, w_in, att_q_gain, att_k_gain, pool_w, pool_scale,
                gla_w_gate, gla_b_gate, gla_out_gain, w_out):
    offs = [0]
    for s in IN_SPLITS:
        offs.append(offs[-1] + s)
    (aq, ak, av, az, iq, ik, iw, pu, pz, gq, gk, gv, gg, gz) = [
        w_in[:, offs[i]:offs[i + 1]] for i in range(len(IN_SPLITS))]
    nat = {
        "q": aq, "k": ak, "z": az, "iq": iq,
        "ik2": jnp.concatenate([ik, ik], axis=1),
        "pu": pu, "pz": pz,
        "gq": _pad_heads(gq, GLA_HEADS, GLA_DK, GLA_DK_PAD, 1),
        "gk": _pad_heads(gk, GLA_HEADS, GLA_DK, GLA_DK_PAD, 1),
        "gv": _pad_heads(gv, GLA_HEADS, GLA_DV, GLA_DV_PAD, 1),
        "gz": _pad_heads(gz, GLA_HEADS, GLA_DV, GLA_DV_PAD, 1),
        "gg": jnp.pad(gg, ((0, 0), (0, GATE_PAD - GLA_GATE_RANK))),
    }
    wn = jnp.concatenate([nat[name] for name, _ in _NAT_PIECES], axis=1).astype(BF16)
    iw_t = jnp.pad(iw.T * (IDX_HEADS ** -0.5), ((0, IW_ROWS - IDX_HEADS), (0, 0)))
    wt = jnp.concatenate([av.T, iw_t], axis=0).astype(BF16)
    qg = (jnp.tile(att_q_gain, ATT_HEADS) * (ATT_HEAD_DIM ** -0.5)).reshape(1, ATT_WIDTH)
    kg = jnp.tile(att_k_gain, ATT_HEADS).reshape(1, ATT_WIDTH)
    hid = jnp.arange(ATT_WIDTH) // ATT_HEAD_DIM
    gm = jnp.where(hid[:, None] == hid[None, :], 1.0 / ATT_HEAD_DIM, 0.0).astype(BF16)
    pw = jax.scipy.linalg.block_diag(*[pool_w[g] for g in range(POOL_GROUPS)]).astype(BF16)
    ps = pool_scale.reshape(1, POOL_WIDTH)
    wg = jnp.pad(_pad_heads(gla_w_gate, GLA_HEADS, GLA_DK, GLA_DK_PAD, 1),
                 ((0, GATE_PAD - GLA_GATE_RANK), (0, 0)))
    bg = _pad_heads(gla_b_gate.reshape(1, -1), GLA_HEADS, GLA_DK, GLA_DK_PAD, 1)
    og = jnp.pad(gla_out_gain, (0, GLA_DV_PAD - GLA_DV)).reshape(1, GLA_DV_PAD)
    wa = w_out[0:ATT_WIDTH].astype(BF16)
    wb = w_out[ATT_WIDTH:ATT_WIDTH + POOL_WIDTH].astype(BF16)
    wc = _pad_heads(w_out[ATT_WIDTH + POOL_WIDTH:], GLA_HEADS, GLA_DV, GLA_DV_PAD, 0).astype(BF16)
    return dict(ng=norm_g.reshape(1, D_MODEL), wn=wn, wt=wt, qg=qg, kg=kg, gm=gm,
                pw=pw, ps=ps, wg=wg, bg=bg, og=og, wa=wa, wb=wb, wc=wc)


def _layer(x2, p, *, B, S, tm, tq, kc, topk):
    (q, k, gaz, iq, ik, pu, gpz, gq, gk, gv, ggz, gg, vt, iwt) = _proj_call(
        x2, p["ng"], p["wn"], p["wt"], p["qg"], p["kg"], p["gm"], tm=tm, kc=kc)
    r3 = lambda a: a.reshape(B, S, a.shape[-1])
    ya = _attn_call(r3(q), r3(k), vt.reshape(B, S // kc, VT_ROWS, kc), r3(iq), r3(ik),
                    iwt.reshape(IW_ROWS, B, S).transpose(1, 0, 2), r3(gaz), tq=tq, kc=kc, topk=topk)
    yb, yc = _mix_call(r3(pu), r3(gpz), r3(gq), r3(gk), r3(gv), r3(ggz), r3(gg),
                       p["pw"], p["ps"], p["wg"], p["bg"], p["og"])
    T = B * S
    return _out_call(x2, ya.reshape(T, ATT_WIDTH), yb.reshape(T, POOL_WIDTH),
                     yc.reshape(T, GLA_VAL_PAD), p["wa"], p["wb"], p["wc"], tm=tm)


def _tiles(S):
    tq = min(512, S // 2)
    kc = min(256, tq)
    tm = min(512, S)
    return tm, tq, kc


def kernel(x, norm_g, w_in, att_q_gain, att_k_gain, pool_w, pool_scale, gla_w_gate,
           gla_b_gate, gla_out_gain, w_out):
    B, S, _ = x.shape
    depth = norm_g.shape[0]
    tm, tq, kc = _tiles(S)
    topk = min(TOPK_MAX, S // 4)
    x2 = x.reshape(B * S, D_MODEL)
    for l in range(depth):
        p = _pack_layer(norm_g[l], w_in[l], att_q_gain[l], att_k_gain[l], pool_w[l],
                        pool_scale[l], gla_w_gate[l], gla_b_gate[l], gla_out_gain[l], w_out[l])
        x2 = _layer(x2, p, B=B, S=S, tm=tm, tq=tq, kc=kc, topk=topk)
    return x2.reshape(B, S, D_MODEL)
```
